```python
import math
import jax
import jax.numpy as jnp
from jax import lax
import numpy as np

D_MODEL = 4096
BATCH = 2
SEQ = 4096
DEPTH = 4

CHUNK = 64
N_MIXERS = 3
D_FF = 4 * D_MODEL
EPS = 1e-6

A_HEADS = 8
A_QK_DIM = D_MODEL // 2 // A_HEADS
A_V_DIM = D_MODEL // A_HEADS
A_GATE_CAP = 15.0
A_IN = 2 * A_HEADS * A_QK_DIM + 2 * D_MODEL + 2 * A_HEADS

B_HEADS = 32
B_HEAD_DIM = D_MODEL // B_HEADS
Q_BLOCK = 128

C_HEAD_DIM = 64
C_HEADS = D_MODEL // C_HEAD_DIM
C_DECAY_LORA = max(32, int(round(1.8 * D_MODEL ** 0.5 / 32)) * 32)
C_AAA_LORA = max(32, int(round(1.8 * D_MODEL ** 0.5 / 32)) * 32)
C_GATE_LORA = max(32, int(round(0.6 * D_MODEL ** 0.8 / 32)) * 32)
C_GN_EPS = 64e-5

N_A = (DEPTH + 2) // 3
N_B = (DEPTH + 1) // 3
N_C = DEPTH // 3

kernel_name = 'hybrid_mlstm_stickbreak_rwkv7_trunk'


def rms_norm(x, g):
    xf = x.astype(jnp.float32)
    y = xf * lax.rsqrt(jnp.mean(xf * xf, axis=-1, keepdims=True) + EPS)
    return (y * g.astype(jnp.float32)).astype(x.dtype)


def squared_relu_mlp(x, w1, w2):
    return jnp.square(jax.nn.relu(x @ w1)) @ w2


def _mlstm_chunk_step(carry, xs):
    c_state, n_state, m_state = carry
    q, k, v, ig, logf = xs
    length = q.shape[-2]
    b = jnp.cumsum(logf, axis=-1)
    causal = jnp.tril(jnp.ones((length, length), dtype=bool))
    d = jnp.where(causal, b[..., :, None] - b[..., None, :] + ig[..., None, :], -jnp.inf)
    inter = b + m_state[..., None]
    m_t = jnp.maximum(jnp.max(d, axis=-1), inter)
    s = jnp.einsum('bhtd,bhsd->bhts', q, k) * jnp.exp(d - m_t[..., None])
    inter_w = jnp.exp(inter - m_t)
    num = jnp.einsum('bhts,bhsv->bhtv', s, v) + inter_w[..., None] * jnp.einsum('bhtd,bhdv->bhtv', q, c_state)
    den = jnp.sum(s, axis=-1) + inter_w * jnp.einsum('bhtd,bhd->bht', q, n_state)
    h = num / jnp.maximum(jnp.abs(den), jnp.exp(-m_t))[..., None]
    b_last = b[..., -1]
    a = b_last[..., None] - b + ig
    m_new = jnp.maximum(b_last + m_state, jnp.max(a, axis=-1))
    w = jnp.exp(a - m_new[..., None])
    decay = jnp.exp(b_last + m_state - m_new)
    c_new = decay[..., None, None] * c_state + jnp.einsum('bhsd,bhsv->bhdv', k * w[..., None], v)
    n_new = decay[..., None] * n_state + jnp.einsum('bhs,bhsd->bhd', w, k)
    return (c_new, n_new, m_new), h


def mlstm_mixer(x, w_in, b_if, norm_g, w_out):
    bsz, seq, _ = x.shape
    nc = seq // CHUNK
    qk = A_HEADS * A_QK_DIM
    proj = x @ w_in
    q = proj[..., :qk]
    k = proj[..., qk:2 * qk]
    v = proj[..., 2 * qk:2 * qk + D_MODEL]
    o_pre = proj[..., 2 * qk + D_MODEL:2 * qk + 2 * D_MODEL]
    gates = proj[..., 2 * qk + 2 * D_MODEL:].astype(jnp.float32) + b_if.astype(jnp.float32)
    ig = A_GATE_CAP * jnp.tanh(gates[..., :A_HEADS] / A_GATE_CAP)
    logf = jax.nn.log_sigmoid(gates[..., A_HEADS:])

    def to_chunks(t, dh):
        return t.astype(jnp.float32).reshape(bsz, nc, CHUNK, A_HEADS, dh).transpose(1, 0, 3, 2, 4)

    def gate_chunks(g):
        return g.reshape(bsz, nc, CHUNK, A_HEADS).transpose(1, 0, 3, 2)

    qc = to_chunks(q, A_QK_DIM) * (A_QK_DIM ** -0.5)
    kc = to_chunks(k, A_QK_DIM)
    vc = to_chunks(v, A_V_DIM)
    init = (jnp.zeros((bsz, A_HEADS, A_QK_DIM, A_V_DIM), jnp.float32),
            jnp.zeros((bsz, A_HEADS, A_QK_DIM), jnp.float32),
            jnp.zeros((bsz, A_HEADS), jnp.float32))
    _, h = lax.scan(_mlstm_chunk_step, init, (qc, kc, vc, gate_chunks(ig), gate_chunks(logf)))
    h = h.transpose(1, 0, 3, 2, 4).reshape(bsz, seq, A_HEADS, A_V_DIM)
    h = h * lax.rsqrt(jnp.mean(h * h, axis=-1, keepdims=True) + EPS)
    h = (h.reshape(bsz, seq, D_MODEL) * norm_g.astype(jnp.float32)).astype(x.dtype)
    return (h * jax.nn.sigmoid(o_pre)) @ w_out


def stick_breaking_mixer(x, w_in, w_out):
    bsz, seq, _ = x.shape
    qkv = (x @ w_in).reshape(bsz, seq, 3, B_HEADS, B_HEAD_DIM)
    q = qkv[:, :, 0].transpose(0, 2, 1, 3)
    k = qkv[:, :, 1].transpose(0, 2, 1, 3)
    v = qkv[:, :, 2].transpose(0, 2, 1, 3)
    scale = B_HEAD_DIM ** -0.5
    outs = []
    for blk in range(seq // Q_BLOCK):
        start = blk * Q_BLOCK
        end = start + Q_BLOCK
        z = jnp.einsum('bhtd,bhsd->bhts', q[:, :, start:end], k[:, :, :end]).astype(jnp.float32) * scale
        t_pos = start + jnp.arange(Q_BLOCK)
        s_pos = jnp.arange(end)
        strict = s_pos[None, :] < t_pos[:, None]
        log_keep = jnp.where(strict, jax.nn.log_sigmoid(-z), 0.0)
        after = lax.cumsum(log_keep, axis=3, reverse=True) - log_keep
        weight = jnp.where(strict, jnp.exp(jax.nn.log_sigmoid(z) + after), 0.0)
        outs.append(jnp.einsum('bhts,bhsv->bhtv', weight.astype(v.dtype), v[:, :, :end]))
    o = jnp.concatenate(outs, axis=2).transpose(0, 2, 1, 3).reshape(bsz, seq, D_MODEL)
    return o @ w_out


def _rwkv7_step(state, xs):
    r, decay, k, v, kk, a = xs
    sk = jnp.einsum('bhvk,bhk->bhv', state, kk)
    state = (state * decay[:, :, None, :]
             - sk[..., None] * (kk * a)[:, :, None, :]
             + v[..., None] * k[:, :, None, :])
    y = jnp.einsum('bhvk,bhk->bhv', state, r)
    return state, y


def rwkv7_mixer(x, mu, w_rkv, w0, w1, w2, a0, a1, a2, g1, g2, k_k, k_a, r_k, gn_g, gn_b, w_out):
    bsz, seq, _ = x.shape
    x_prev = jnp.pad(x, ((0, 0), (1, 0), (0, 0)))[:, :-1]
    xx = x_prev - x
    xr = x + xx * mu[0]
    xw = x + xx * mu[1]
    xk = x + xx * mu[2]
    xv = x + xx * mu[3]
    xa = x + xx * mu[4]
    xg = x + xx * mu[5]
    r = xr @ w_rkv[0]
    k = xk @ w_rkv[1]
    v = xv @ w_rkv[2]
    w_log = (w0 + jnp.tanh(xw @ w1) @ w2).astype(jnp.float32)
    w_log = -jax.nn.softplus(-w_log) - 0.5
    decay = jnp.exp(-jnp.exp(w_log))
    a = jax.nn.sigmoid((a0 + (xa @ a1) @ a2).astype(jnp.float32))
    g = jax.nn.sigmoid(xg @ g1) @ g2

    def heads(t):
        return t.astype(jnp.float32).reshape(bsz, seq, C_HEADS, C_HEAD_DIM)

    kk = heads(k * k_k)
    kk = kk * lax.rsqrt(jnp.maximum(jnp.sum(kk * kk, axis=-1, keepdims=True), 1e-24))
    a_h = heads(a)
    k_a_h = k_a.astype(jnp.float32).reshape(C_HEADS, C_HEAD_DIM)
    k_h = heads(k) * (1.0 + (a_h - 1.0) * k_a_h)
    r_h = heads(r)
    v_h = heads(v)

    def tm(t):
        return t.transpose(1, 0, 2, 3)

    init = jnp.zeros((bsz, C_HEADS, C_HEAD_DIM, C_HEAD_DIM), jnp.float32)
    _, y = lax.scan(_rwkv7_step, init, (tm(r_h), tm(heads(decay)), tm(k_h), tm(v_h), tm(kk), tm(a_h)))
    y = y.transpose(1, 0, 2, 3)
    mean = jnp.mean(y, axis=-1, keepdims=True)
    var = jnp.mean(jnp.square(y - mean), axis=-1, keepdims=True)
    y = (y - mean) * lax.rsqrt(var + C_GN_EPS)
    y = y.reshape(bsz, seq, D_MODEL) * gn_g.astype(jnp.float32) + gn_b.astype(jnp.float32)
    bonus = jnp.sum(r_h * k_h * r_k.astype(jnp.float32), axis=-1, keepdims=True) * v_h
    y = (y + bonus.reshape(bsz, seq, D_MODEL)).astype(x.dtype)
    return (y * g) @ w_out


def setup_inputs(seed: int = 0) -> dict:
    key = jax.random.key(seed)
    ks = jax.random.split(key, 32)
    f32 = jnp.float32
    d = D_MODEL

    def nrm(k, shape, scale):
        return jax.random.normal(k, shape, f32) * scale

    return {
        'x': nrm(ks[0], (BATCH, SEQ, d), 1.0),
        'norm_g': 1.0 + nrm(ks[1], (DEPTH, 4, d), 0.05),
        'mlp_w1': nrm(ks[2], (DEPTH, d, D_FF), d ** -0.5),
        'mlp_w2': nrm(ks[3], (DEPTH, D_FF, d), D_FF ** -0.5),
        'a_w_in': nrm(ks[4], (N_A, d, A_IN), d ** -0.5),
        'a_b_if': jnp.concatenate([nrm(ks[5], (N_A, A_HEADS), 0.1),
                                   3.0 + nrm(ks[6], (N_A, A_HEADS), 0.5)], axis=-1),
        'a_norm_g': 1.0 + nrm(ks[7], (N_A, d), 0.05),
        'a_w_out': nrm(ks[8], (N_A, d, d), d ** -0.5),
        'b_w_in': nrm(ks[9], (N_B, d, 3 * d), d ** -0.5),
        'b_w_out': nrm(ks[10], (N_B, d, d), d ** -0.5),
        'c_mu': jax.random.uniform(ks[11], (N_C, 6, d), f32),
        'c_w_rkv': nrm(ks[12], (N_C, 3, d, d), d ** -0.5),
        'c_w0': jax.random.uniform(ks[13], (N_C, d), f32, -6.5, -1.5),
        'c_w1': nrm(ks[14], (N_C, d, C_DECAY_LORA), d ** -0.5),
        'c_w2': nrm(ks[15], (N_C, C_DECAY_LORA, d), 0.1 * C_DECAY_LORA ** -0.5),
        'c_a0': nrm(ks[16], (N_C, d), 0.1),
        'c_a1': nrm(ks[17], (N_C, d, C_AAA_LORA), d ** -0.5),
        'c_a2': nrm(ks[18], (N_C, C_AAA_LORA, d), C_AAA_LORA ** -0.5),
        'c_g1': nrm(ks[19], (N_C, d, C_GATE_LORA), d ** -0.5),
        'c_g2': nrm(ks[20], (N_C, C_GATE_LORA, d), C_GATE_LORA ** -0.5),
        'c_k_k': 0.85 + nrm(ks[21], (N_C, d), 0.05),
        'c_k_a': 1.0 + nrm(ks[22], (N_C, d), 0.05),
        'c_r_k': nrm(ks[23], (N_C, C_HEADS, C_HEAD_DIM), 0.1),
        'c_gn_g': 1.0 + nrm(ks[24], (N_C, d), 0.05),
        'c_gn_b': nrm(ks[25], (N_C, d), 0.01),
        'c_w_out': nrm(ks[26], (N_C, d, d), d ** -0.5),
    }


def reference(x, norm_g, mlp_w1, mlp_w2, a_w_in, a_b_if, a_norm_g, a_w_out, b_w_in, b_w_out,
              c_mu, c_w_rkv, c_w0, c_w1, c_w2, c_a0, c_a1, c_a2, c_g1, c_g2, c_k_k, c_k_a,
              c_r_k, c_gn_g, c_gn_b, c_w_out):
    h = x
    for layer in range(DEPTH):
        kind = layer % N_MIXERS
        idx = layer // N_MIXERS
        u = rms_norm(h, norm_g[layer, 0])
        if kind == 0:
            u = mlstm_mixer(u, a_w_in[idx], a_b_if[idx], a_norm_g[idx], a_w_out[idx])
        elif kind == 1:
            u = stick_breaking_mixer(u, b_w_in[idx], b_w_out[idx])
        else:
            u = rwkv7_mixer(u, c_mu[idx], c_w_rkv[idx], c_w0[idx], c_w1[idx], c_w2[idx],
                            c_a0[idx], c_a1[idx], c_a2[idx], c_g1[idx], c_g2[idx],
                            c_k_k[idx], c_k_a[idx], c_r_k[idx], c_gn_g[idx], c_gn_b[idx],
                            c_w_out[idx])
        h = h + rms_norm(u, norm_g[layer, 1])
        u = squared_relu_mlp(rms_norm(h, norm_g[layer, 2]), mlp_w1[layer], mlp_w2[layer])
        h = h + rms_norm(u, norm_g[layer, 3])
    return h
```

```python
import functools
import math

import jax
import jax.numpy as jnp
from jax import lax
from jax.experimental import pallas as pl
from jax.experimental.pallas import tpu as pltpu

F32 = jnp.float32
BF16 = jnp.bfloat16

EPS = 1e-6
VMEM_LIMIT_BYTES = 56 * 1024 * 1024
MM_TILE_BYTES = 8 * 1024 * 1024

A_HEADS = 8
A_GATE_CAP = 15.0
B_HEADS = 32
Q_BLOCK = 128
C_HEAD_DIM = 64
C_GN_EPS = 64e-5


def _params(*sem):
    return pltpu.CompilerParams(dimension_semantics=sem, vmem_limit_bytes=VMEM_LIMIT_BYTES)


def _log_sigmoid(z):
    return jnp.minimum(z, 0.0) - jnp.log1p(jnp.exp(-jnp.abs(z)))


def _sigmoid(z):
    return 1.0 / (1.0 + jnp.exp(-z))


_ACTS = {
    "none": lambda r: r,
    "relu2": lambda r: jnp.square(jnp.maximum(r, 0.0)),
    "sigmoid": _sigmoid,
    "tanh": jnp.tanh,
}


def _mm_kernel(a_ref, w_ref, o_ref, *scratch, act, nk):
    part = jnp.dot(a_ref[...].astype(BF16), w_ref[...].astype(BF16), preferred_element_type=F32)
    if nk == 1:
        o_ref[...] = _ACTS[act](part).astype(o_ref.dtype)
        return
    acc_ref, = scratch
    k = pl.program_id(2)

    @pl.when(k == 0)
    def _():
        acc_ref[...] = part

    @pl.when(k > 0)
    def _():
        acc_ref[...] += part

    @pl.when(k == nk - 1)
    def _():
        o_ref[...] = _ACTS[act](acc_ref[...]).astype(o_ref.dtype)


def matmul(a, w, *, act="none", out_dtype=F32, tm=1024, tn=1024, tk=4096):
    m, k = a.shape
    k2, n = w.shape
    assert k == k2
    tm, tn = min(tm, m), min(tn, n)
    tk = min(tk, k, MM_TILE_BYTES // (tm * a.dtype.itemsize))
    assert m % tm == 0 and n % tn == 0 and k % tk == 0, (a.shape, w.shape)
    nk = k // tk
    return pl.pallas_call(
        functools.partial(_mm_kernel, act=act, nk=nk),
        grid=(m // tm, n // tn, nk),
        in_specs=[pl.BlockSpec((tm, tk), lambda i, j, kk: (i, kk)),
                  pl.BlockSpec((tk, tn), lambda i, j, kk: (kk, j))],
        out_specs=pl.BlockSpec((tm, tn), lambda i, j, kk: (i, j)),
        out_shape=jax.ShapeDtypeStruct((m, n), out_dtype),
        scratch_shapes=[pltpu.VMEM((tm, tn), F32)] if nk > 1 else [],
        compiler_params=_params("parallel", "parallel", "arbitrary"),
    )(a, w)


def _rms(x, g):
    return x * lax.rsqrt(jnp.mean(x * x, axis=-1, keepdims=True) + EPS) * g


def _rms_kernel(x_ref, g_ref, o_ref):
    o_ref[...] = _rms(x_ref[...], g_ref[...]).astype(o_ref.dtype)


def rms_norm(x, g, *, out_dtype=BF16, tr=256):
    m, d = x.shape
    tr = min(tr, m)
    return pl.pallas_call(
        _rms_kernel,
        grid=(m // tr,),
        in_specs=[pl.BlockSpec((tr, d), lambda i: (i, 0)), pl.BlockSpec((1, d), lambda i: (0, 0))],
        out_specs=pl.BlockSpec((tr, d), lambda i: (i, 0)),
        out_shape=jax.ShapeDtypeStruct((m, d), out_dtype),
        compiler_params=_params("parallel"),
    )(x, g.reshape(1, d))


def _add_rms_kernel(h_ref, u_ref, gpost_ref, *rest, with_pre):
    hn = h_ref[...] + _rms(u_ref[...].astype(F32), gpost_ref[...])
    if with_pre:
        gpre_ref, hn_ref, y_ref = rest
        y_ref[...] = _rms(hn, gpre_ref[...]).astype(y_ref.dtype)
    else:
        hn_ref, = rest
    hn_ref[...] = hn


def add_rms(h, u, g_post, g_pre=None, *, pre_dtype=BF16, tr=128):
    m, d = h.shape
    tr = min(tr, m)
    with_pre = g_pre is not None
    row = pl.BlockSpec((tr, d), lambda i: (i, 0))
    vec = pl.BlockSpec((1, d), lambda i: (0, 0))
    args = [h, u, g_post.reshape(1, d)] + ([g_pre.reshape(1, d)] if with_pre else [])
    out_shape = [jax.ShapeDtypeStruct((m, d), F32)] + ([jax.ShapeDtypeStruct((m, d), pre_dtype)] if with_pre else [])
    outs = pl.pallas_call(
        functools.partial(_add_rms_kernel, with_pre=with_pre),
        grid=(m // tr,),
        in_specs=[row, row, vec] + ([vec] if with_pre else []),
        out_specs=[row] * len(out_shape),
        out_shape=out_shape,
        compiler_params=_params("parallel"),
    )(*args)
    return outs if with_pre else (outs[0], None)


def _mlstm_gate_kernel(x_ref, w_ref, b_ref, o_ref):
    g = jnp.dot(x_ref[...], w_ref[...], preferred_element_type=F32) + b_ref[...]
    lane = lax.broadcasted_iota(jnp.int32, g.shape, 1)
    ig = A_GATE_CAP * jnp.tanh(g / A_GATE_CAP)
    o_ref[...] = jnp.where(lane < A_HEADS, ig, _log_sigmoid(g))


def mlstm_gates(x, w_gate, b_if, *, tr=512):
    m, d = x.shape
    tr = min(tr, m)
    ng = 128
    w = jnp.zeros((d, ng), BF16).at[:, :2 * A_HEADS].set(w_gate.astype(BF16))
    b = jnp.zeros((1, ng), F32).at[0, :2 * A_HEADS].set(b_if.astype(F32))
    return pl.pallas_call(
        _mlstm_gate_kernel,
        grid=(m // tr,),
        in_specs=[pl.BlockSpec((tr, d), lambda i: (i, 0)), pl.BlockSpec((d, ng), lambda i: (0, 0)),
                  pl.BlockSpec((1, ng), lambda i: (0, 0))],
        out_specs=pl.BlockSpec((tr, ng), lambda i: (i, 0)),
        out_shape=jax.ShapeDtypeStruct((m, ng), F32),
        compiler_params=_params("parallel"),
    )(x, w, b)


def _mlstm_kernel(q_ref, k_ref, v_ref, og_ref, igc_ref, igr_ref, lfc_ref, lfr_ref, g_ref,
                  out_ref, c_ref, n_ref, m_ref, *, L, qk_scale):
    @pl.when(pl.program_id(2) == 0)
    def _():
        c_ref[...] = jnp.zeros_like(c_ref)
        n_ref[...] = jnp.zeros_like(n_ref)
        m_ref[...] = jnp.zeros_like(m_ref)

    q = q_ref[...] * qk_scale
    k = k_ref[...]
    v = v_ref[...]
    ig_col, ig_row = igc_ref[...], igr_ref[...]
    lf_col, lf_row = lfc_ref[...], lfr_ref[...]
    m_prev = m_ref[...]

    row = lax.broadcasted_iota(jnp.int32, (L, L), 0)
    col = lax.broadcasted_iota(jnp.int32, (L, L), 1)
    causal = col <= row
    tri = causal.astype(F32)
    b_col = jnp.sum(tri * lf_row, axis=1, keepdims=True)
    b_row = jnp.sum((row <= col).astype(F32) * lf_col, axis=0, keepdims=True)
    b_last = jnp.sum(lf_row, axis=1, keepdims=True)

    d = jnp.where(causal, b_col - b_row + ig_row, -jnp.inf)
    inter = b_col + m_prev
    m_t = jnp.maximum(jnp.max(d, axis=1, keepdims=True), inter)
    s = lax.dot_general(q, k, (((1,), (1,)), ((), ())), preferred_element_type=F32) * jnp.exp(d - m_t)
    inter_w = jnp.exp(inter - m_t)
    c_prev = c_ref[...]
    n_prev = n_ref[...]
    num = (jnp.dot(s.astype(BF16), v, preferred_element_type=F32)
           + inter_w * jnp.dot(q, c_prev.astype(BF16), preferred_element_type=F32))
    qn = jnp.sum(q.astype(F32) * n_prev, axis=1, keepdims=True)
    den = jnp.sum(s, axis=1, keepdims=True) + inter_w * qn
    h = num / jnp.maximum(jnp.abs(den), jnp.exp(-m_t))

    a_col = b_last - b_col + ig_col
    a_row = b_last - b_row + ig_row
    m_new = jnp.maximum(b_last + m_prev, jnp.max(a_row, axis=1, keepdims=True))
    w_col = jnp.exp(a_col - m_new)
    decay = jnp.exp(b_last + m_prev - m_new)
    kw = k.astype(F32) * w_col
    c_ref[...] = decay * c_prev + lax.dot_general(
        kw.astype(BF16), v, (((0,), (0,)), ((), ())), preferred_element_type=F32)
    n_ref[...] = decay * n_prev + jnp.sum(kw, axis=0, keepdims=True)
    m_ref[...] = m_new

    hn = h * lax.rsqrt(jnp.mean(h * h, axis=1, keepdims=True) + EPS) * g_ref[...]
    out_ref[...] = (hn * _sigmoid(og_ref[...].astype(F32))).astype(out_ref.dtype)


def mlstm_scan(proj, gates, norm_g, *, bsz, seq, L=64):
    h_ = A_HEADS
    d_model = norm_g.shape[0]
    dv = d_model // h_
    dqk = dv // 2
    nc = seq // L
    g4 = gates[:, :2 * h_].reshape(bsz, nc, L, 2 * h_).transpose(0, 3, 1, 2)
    ig, lf = g4[:, :h_], g4[:, h_:]
    col = lambda t: t[..., None]
    rowv = lambda t: t[..., None, :]
    col_spec = pl.BlockSpec((None, None, None, L, 1), lambda b, h, c: (b, h, c, 0, 0))
    row_spec = pl.BlockSpec((None, None, None, 1, L), lambda b, h, c: (b, h, c, 0, 0))
    kq = 2 * h_ * dqk // dv
    return pl.pallas_call(
        functools.partial(_mlstm_kernel, L=L, qk_scale=dqk ** -0.5),
        grid=(bsz, h_, nc),
        in_specs=[pl.BlockSpec((L, dqk), lambda b, h, c: (b * nc + c, h)),
                  pl.BlockSpec((L, dqk), lambda b, h, c: (b * nc + c, h_ + h)),
                  pl.BlockSpec((L, dv), lambda b, h, c: (b * nc + c, kq + h)),
                  pl.BlockSpec((L, dv), lambda b, h, c: (b * nc + c, kq + h_ + h)),
                  col_spec, row_spec, col_spec, row_spec,
                  pl.BlockSpec((1, dv), lambda b, h, c: (0, h))],
        out_specs=pl.BlockSpec((L, dv), lambda b, h, c: (b * nc + c, h)),
        out_shape=jax.ShapeDtypeStruct((bsz * seq, d_model), BF16),
        scratch_shapes=[pltpu.VMEM((dqk, dv), F32), pltpu.VMEM((1, dqk), F32), pltpu.VMEM((1, 1), F32)],
        compiler_params=_params("parallel", "parallel", "arbitrary"),
    )(proj, proj, proj, proj, col(ig), rowv(ig), col(lf), rowv(lf), norm_g.reshape(1, d_model).astype(F32))


def mlstm_mixer(x, w_in, b_if, norm_g, w_out, *, bsz, seq):
    d_model = x.shape[1]
    n_main = w_in.shape[1] - 2 * A_HEADS
    proj = matmul(x, w_in[:, :n_main].astype(BF16), out_dtype=BF16)
    gates = mlstm_gates(x, w_in[:, n_main:], b_if)
    hg = mlstm_scan(proj, gates, norm_g, bsz=bsz, seq=seq)
    return matmul(hg, w_out.astype(BF16))


def _sb_kernel(q_ref, k_ref, v_ref, o_ref, *, scale):
    T = Q_BLOCK
    qi = pl.program_id(2)
    q = q_ref[...]
    row = lax.broadcasted_iota(jnp.int32, (T, T), 0)
    col = lax.broadcasted_iota(jnp.int32, (T, T), 1)
    strict = col < row
    upper = (row > col).astype(BF16)
    upper2 = jnp.concatenate([upper, upper], axis=0)

    def block(kb, carry, acc, masked):
        start = pl.multiple_of(kb * T, T)
        k = k_ref[pl.ds(start, T), :]
        v = v_ref[pl.ds(start, T), :]
        z = lax.dot_general(q, k, (((1,), (1,)), ((), ())), preferred_element_type=F32) * scale
        sp = jnp.log1p(jnp.exp(-jnp.abs(z)))
        ls_pos = jnp.minimum(z, 0.0) - sp
        lk = jnp.minimum(-z, 0.0) - sp
        if masked:
            lk = jnp.where(strict, lk, 0.0)
        hi = lk.astype(BF16)
        lo = (lk - hi.astype(F32)).astype(BF16)
        after = jnp.dot(jnp.concatenate([hi, lo], axis=1), upper2, preferred_element_type=F32)
        w = jnp.exp(ls_pos + after + carry)
        if masked:
            w = jnp.where(strict, w, 0.0)
        acc = acc + jnp.dot(w.astype(BF16), v, preferred_element_type=F32)
        carry = carry + jnp.sum(lk, axis=1, keepdims=True)
        return carry, acc

    carry, acc = block(qi, jnp.zeros((T, 1), F32), jnp.zeros(q.shape, F32), True)

    def body(j, ca):
        return block(qi - 1 - j, ca[0], ca[1], False)

    carry, acc = lax.fori_loop(0, qi, body, (carry, acc))
    o_ref[...] = acc.astype(o_ref.dtype)


def stick_breaking_scan(qkv, *, bsz, seq):
    d_model = qkv.shape[1] // 3
    dh = d_model // B_HEADS
    nq = seq // Q_BLOCK
    return pl.pallas_call(
        functools.partial(_sb_kernel, scale=dh ** -0.5),
        grid=(bsz, B_HEADS, nq),
        in_specs=[pl.BlockSpec((Q_BLOCK, dh), lambda b, h, i: (b * nq + i, h)),
                  pl.BlockSpec((seq, dh), lambda b, h, i: (b, B_HEADS + h)),
                  pl.BlockSpec((seq, dh), lambda b, h, i: (b, 2 * B_HEADS + h))],
        out_specs=pl.BlockSpec((Q_BLOCK, dh), lambda b, h, i: (b * nq + i, h)),
        out_shape=jax.ShapeDtypeStruct((bsz * seq, d_model), BF16),
        compiler_params=_params("parallel", "parallel", "arbitrary"),
    )(qkv, qkv, qkv)


def stick_breaking_mixer(x, w_in, w_out, *, bsz, seq):
    qkv = matmul(x, w_in.astype(BF16), out_dtype=BF16)
    o = stick_breaking_scan(qkv, bsz=bsz, seq=seq)
    return matmul(o, w_out.astype(BF16))


def _shift_kernel(x_ref, prev_ref, mu_ref, *o_refs, tiles_per_seq):
    x = x_ref[...]
    first = pl.program_id(0) % tiles_per_seq == 0
    prev_last = jnp.where(first, 0.0, prev_ref[7:8, :])
    row = lax.broadcasted_iota(jnp.int32, x.shape, 0)
    x_prev = jnp.where(row == 0, prev_last, pltpu.roll(x, 1, 0))
    xx = x_prev - x
    for i, o_ref in enumerate(o_refs):
        o_ref[...] = (x + xx * mu_ref[i:i + 1, :]).astype(o_ref.dtype)


def token_shift_mix(x, mu, *, seq, tr=256):
    m, d = x.shape
    tr = min(tr, seq)
    n_out = mu.shape[0]
    row = pl.BlockSpec((tr, d), lambda i: (i, 0))
    return pl.pallas_call(
        functools.partial(_shift_kernel, tiles_per_seq=seq // tr),
        grid=(m // tr,),
        in_specs=[row, pl.BlockSpec((8, d), lambda i: (jnp.maximum(i * (tr // 8) - 1, 0), 0)),
                  pl.BlockSpec((n_out, d), lambda i: (0, 0))],
        out_specs=[row] * n_out,
        out_shape=[jax.ShapeDtypeStruct((m, d), BF16)] * n_out,
        compiler_params=_params("parallel"),
    )(x, x, mu)


def _split_bf16(x, terms):
    parts = []
    for _ in range(terms - 1):
        p = x.astype(BF16)
        parts.append(p)
        x = x - p.astype(F32)
    parts.append(x.astype(BF16))
    return parts


def _rwkv_kernel(r_ref, k_ref, v_ref, wl_ref, al_ref, g_ref, w0_ref, a0_ref, kkw_ref, ka_ref, rk_ref,
                 gng_ref, gnb_ref, o_ref, s_ref, skt_ref, rt_ref, nb_ref, ypart_ref, mbk_ref, bk_ref,
                 gam_ref, bon_ref, *, L, G):
    N = C_HEAD_DIM
    P = 2 * N
    assert L == N and P == 128

    @pl.when(pl.program_id(2) == 0)
    def _():
        s_ref[...] = jnp.zeros_like(s_ref)

    r128 = lax.broadcasted_iota(jnp.int32, (P, P), 0)
    c128 = lax.broadcasted_iota(jnp.int32, (P, P), 1)
    same_head = (r128 // N) == (c128 // N)
    ones_blk = same_head.astype(BF16)
    ones2 = jnp.concatenate([ones_blk, ones_blk], axis=0)
    rl = lax.broadcasted_iota(jnp.int32, (L, P), 0)
    cl_ = lax.broadcasted_iota(jnp.int32, (L, P), 1) % N
    strict = cl_ < rl
    incl = cl_ <= rl
    tri = (lax.broadcasted_iota(jnp.int32, (L, L), 1) <= lax.broadcasted_iota(jnp.int32, (L, L), 0)).astype(BF16)
    lane_head0 = lax.broadcasted_iota(jnp.int32, (1, P), 1) < N

    def seg_sum(x):
        hi, lo = _split_bf16(x, 2)
        return jnp.dot(jnp.concatenate([hi, lo], axis=1), ones2, preferred_element_type=F32)

    for g in range(G):
        sl = slice(g * P, (g + 1) * P)
        r = r_ref[:, sl]
        k = k_ref[:, sl]
        v = v_ref[:, sl]
        wraw = wl_ref[:, sl] + w0_ref[:, sl]
        w_log = -(jnp.maximum(-wraw, 0.0) + jnp.log1p(jnp.exp(-jnp.abs(wraw)))) - 0.5
        ld = -jnp.exp(w_log)
        a = _sigmoid(al_ref[:, sl] + a0_ref[:, sl])
        kkr = k * kkw_ref[:, sl]
        kk = kkr * lax.rsqrt(jnp.maximum(seg_sum(kkr * kkr), 1e-24))
        kh = k * (1.0 + (a - 1.0) * ka_ref[:, sl])
        b = -(kk * a)
        bon_ref[g] = seg_sum(r * kh * rk_ref[:, sl])

        cl3 = jnp.dot(tri, jnp.concatenate(_split_bf16(ld, 3), axis=1), preferred_element_type=F32)
        cl = cl3[:, :P] + cl3[:, P:2 * P] + cl3[:, 2 * P:]
        cl_last = cl[L - 1:L, :]
        e_inv = jnp.exp(-cl)
        e_tail = jnp.exp(cl_last - cl)
        kkt = kk * jnp.exp(cl - ld)
        rt = r * jnp.exp(cl)
        lhs = jnp.concatenate([kkt, rt], axis=0)
        rhs = jnp.concatenate([b * e_inv, kh * e_inv], axis=0).astype(BF16)
        nt = (((1,), (1,)), ((), ()))
        res0 = lax.dot_general(jnp.where(lane_head0, lhs, 0.0).astype(BF16), rhs, nt, preferred_element_type=F32)
        res1 = lax.dot_general(jnp.where(lane_head0, 0.0, lhs).astype(BF16), rhs, nt, preferred_element_type=F32)
        pair = lambda blk_r, blk_c: jnp.concatenate(
            [res0[blk_r * L:(blk_r + 1) * L, blk_c * L:(blk_c + 1) * L],
             res1[blk_r * L:(blk_r + 1) * L, blk_c * L:(blk_c + 1) * L]], axis=1)
        nb = jnp.where(strict, pair(0, 0), 0.0)
        nk = jnp.where(strict, pair(0, 1), 0.0)
        mb = jnp.where(incl, pair(1, 0), 0.0)
        mk = jnp.where(incl, pair(1, 1), 0.0)

        s_blk = s_ref[g]
        ls = lax.dot_general(lhs.astype(BF16), s_blk.astype(BF16), nt, preferred_element_type=F32)
        v_blk = jnp.where(same_head, jnp.concatenate([v, v], axis=0), 0.0).astype(BF16)
        rhs_rows = ls[:L] + jnp.dot(nk.astype(BF16), v_blk, preferred_element_type=F32)
        rhs_t = rhs_rows.T
        rt_ref[g] = jnp.concatenate([rhs_t[:N], rhs_t[N:]], axis=1)
        nb_ref[g] = nb
        skt_ref[g] = jnp.zeros((N, P), F32)
        ypart_ref[g] = ls[L:]
        mbk_ref[g] = jnp.concatenate([mb, mk], axis=1).astype(BF16)
        bk_ref[g] = jnp.concatenate([b * e_tail, kh * e_tail], axis=0).astype(BF16)
        gam_ref[g] = jnp.exp(cl_last)

    lane_s = lax.broadcasted_iota(jnp.int32, (N, P), 1) % N

    def step(t, carry):
        at_t = lane_s == t
        for g in range(G):
            skt = skt_ref[g]
            red = seg_sum(skt * nb_ref[g, pl.ds(t, 1), :])
            skt_ref[g] = jnp.where(at_t, rt_ref[g] + red, skt)
        return carry

    lax.fori_loop(0, L, step, 0)

    for g in range(G):
        sl = slice(g * P, (g + 1) * P)
        v = v_ref[:, sl]
        skt_t = skt_ref[g].T
        sk_blk = jnp.where(same_head, jnp.concatenate([skt_t, skt_t], axis=1), 0.0).astype(BF16)
        sk_rows = jnp.concatenate([skt_t[:N], skt_t[N:]], axis=1)
        v_blk = jnp.where(same_head, jnp.concatenate([v, v], axis=0), 0.0).astype(BF16)
        y = ypart_ref[g] + jnp.dot(mbk_ref[g], jnp.concatenate([sk_blk, v_blk], axis=0),
                                   preferred_element_type=F32)
        d_s = lax.dot_general(jnp.concatenate([sk_rows, v], axis=0).astype(BF16), bk_ref[g],
                              (((0,), (0,)), ((), ())), preferred_element_type=F32)
        s_ref[g] = gam_ref[g] * s_ref[g] + jnp.where(same_head, d_s, 0.0)

        mean = seg_sum(y) * (1.0 / N)
        yc = y - mean
        var = seg_sum(yc * yc) * (1.0 / N)
        yn = yc * lax.rsqrt(var + C_GN_EPS) * gng_ref[:, sl] + gnb_ref[:, sl]
        o_ref[:, sl] = ((yn + bon_ref[g] * v) * g_ref[:, sl]).astype(o_ref.dtype)


def rwkv7_scan(r, k, v, wl, al, g, w0, a0, k_k, k_a, r_k, gn_g, gn_b, *, bsz, seq, G=8):
    m, d = r.shape
    L = C_HEAD_DIM
    P = 2 * C_HEAD_DIM
    nc = seq // L
    wide = G * P
    tile = pl.BlockSpec((L, wide), lambda b, j, c: (b * nc + c, j))
    vec = pl.BlockSpec((1, wide), lambda b, j, c: (0, j))
    vecs = [t.reshape(1, d).astype(F32) for t in (w0, a0, k_k, k_a, r_k, gn_g, gn_b)]
    return pl.pallas_call(
        functools.partial(_rwkv_kernel, L=L, G=G),
        grid=(bsz, d // wide, nc),
        in_specs=[tile] * 6 + [vec] * 7,
        out_specs=tile,
        out_shape=jax.ShapeDtypeStruct((m, d), BF16),
        scratch_shapes=[pltpu.VMEM((G, P, P), F32),
                        pltpu.VMEM((G, C_HEAD_DIM, P), F32),
                        pltpu.VMEM((G, C_HEAD_DIM, P), F32),
                        pltpu.VMEM((G, L, P), F32),
                        pltpu.VMEM((G, L, P), F32),
                        pltpu.VMEM((G, L, 2 * P), BF16),
                        pltpu.VMEM((G, 2 * L, P), BF16),
                        pltpu.VMEM((G, 1, P), F32),
                        pltpu.VMEM((G, L, P), F32)],
        compiler_params=_params("parallel", "parallel", "arbitrary"),
    )(r, k, v, wl, al, g, *vecs)


def rwkv7_mixer(x, mu, w_rkv, w0, w1, w2, a0, a1, a2, g1, g2, k_k, k_a, r_k, gn_g, gn_b, w_out, *, bsz, seq):
    xr, xw, xk, xv, xa, xg = token_shift_mix(x, mu, seq=seq)
    r = matmul(xr, w_rkv[0].astype(BF16))
    k = matmul(xk, w_rkv[1].astype(BF16))
    v = matmul(xv, w_rkv[2].astype(BF16))
    wl = matmul(matmul(xw, w1.astype(BF16), act="tanh", out_dtype=BF16), w2.astype(BF16))
    al = matmul(matmul(xa, a1.astype(BF16), out_dtype=BF16), a2.astype(BF16))
    n_g = g1.shape[1]
    n_gp = -(-n_g // 128) * 128
    g1p = jnp.zeros((g1.shape[0], n_gp), BF16).at[:, :n_g].set(g1.astype(BF16))
    g2p = jnp.zeros((n_gp, g2.shape[1]), BF16).at[:n_g, :].set(g2.astype(BF16))
    g = matmul(matmul(xg, g1p, act="sigmoid", out_dtype=BF16), g2p)
    y = rwkv7_scan(r, k, v, wl, al, g, w0, a0, k_k, k_a, r_k, gn_g, gn_b, bsz=bsz, seq=seq)
    return matmul(y, w_out.astype(BF16))


def kernel(x, norm_g, mlp_w1, mlp_w2, a_w_in, a_b_if, a_norm_g, a_w_out, b_w_in, b_w_out, c_mu, c_w_rkv,
           c_w0, c_w1, c_w2, c_a0, c_a1, c_a2, c_g1, c_g2, c_k_k, c_k_a, c_r_k, c_gn_g, c_gn_b, c_w_out):
    bsz, seq, d = x.shape
    depth = norm_g.shape[0]
    h = x.reshape(bsz * seq, d)
    kinds = [layer % 3 for layer in range(depth)]
    pre_dtype = lambda kind: F32 if kind == 2 else BF16
    y = rms_norm(h, norm_g[0, 0], out_dtype=pre_dtype(kinds[0]))
    for layer in range(depth):
        kind, idx = kinds[layer], layer // 3
        if kind == 0:
            u = mlstm_mixer(y, a_w_in[idx], a_b_if[idx], a_norm_g[idx], a_w_out[idx], bsz=bsz, seq=seq)
        elif kind == 1:
            u = stick_breaking_mixer(y, b_w_in[idx], b_w_out[idx], bsz=bsz, seq=seq)
        else:
            u = rwkv7_mixer(y, c_mu[idx], c_w_rkv[idx], c_w0[idx], c_w1[idx], c_w2[idx], c_a0[idx], c_a1[idx],
                            c_a2[idx], c_g1[idx], c_g2[idx], c_k_k[idx], c_k_a[idx], c_r_k[idx], c_gn_g[idx],
                            c_gn_b[idx], c_w_out[idx], bsz=bsz, seq=seq)
        h, y = add_rms(h, u, norm_g[layer, 1], norm_g[layer, 2])
        mid = matmul(y, mlp_w1[layer].astype(BF16), act="relu2", out_dtype=BF16)
        u = matmul(mid, mlp_w2[layer].astype(BF16))
        last = layer == depth - 1
        h, y = add_rms(h, u, norm_g[layer, 3], None if last else norm_g[layer + 1, 0],
                       pre_dtype=BF16 if last else pre_dtype(kinds[layer + 1]))
    return h.reshape(bsz, seq, d)
```

```python
import functools
import math

import jax
import jax.numpy as jnp
from jax import lax
from jax.experimental import pallas as pl
from jax.experimental.pallas import tpu as pltpu

F32 = jnp.float32
BF16 = jnp.bfloat16

EPS = 1e-6
VMEM_LIMIT_BYTES = 56 * 1024 * 1024
MM_TILE_BYTES = 8 * 1024 * 1024

A_HEADS = 8
A_GATE_CAP = 15.0
B_HEADS = 32
Q_BLOCK = 128
C_HEAD_DIM = 64
C_GN_EPS = 64e-5
SB_UNDERFLOW = -104.0


def _params(*sem):
    return pltpu.CompilerParams(dimension_semantics=sem, vmem_limit_bytes=VMEM_LIMIT_BYTES)


def _log_sigmoid(z):
    return jnp.minimum(z, 0.0) - jnp.log1p(jnp.exp(-jnp.abs(z)))


def _sigmoid(z):
    return 1.0 / (1.0 + jnp.exp(-z))


_ACTS = {
    "none": lambda r: r,
    "relu2": lambda r: jnp.square(jnp.maximum(r, 0.0)),
    "sigmoid": _sigmoid,
    "tanh": jnp.tanh,
}


def _mm_kernel(a_ref, w_ref, o_ref, *scratch, act, nk):
    part = jnp.dot(a_ref[...].astype(BF16), w_ref[...].astype(BF16), preferred_element_type=F32)
    if nk == 1:
        o_ref[...] = _ACTS[act](part).astype(o_ref.dtype)
        return
    acc_ref, = scratch
    k = pl.program_id(2)

    @pl.when(k == 0)
    def _():
        acc_ref[...] = part

    @pl.when(k > 0)
    def _():
        acc_ref[...] += part

    @pl.when(k == nk - 1)
    def _():
        o_ref[...] = _ACTS[act](acc_ref[...]).astype(o_ref.dtype)


def matmul(a, w, *, act="none", out_dtype=F32, tm=1024, tn=1024, tk=4096, name="matmul"):
    m, k = a.shape
    k2, n = w.shape
    assert k == k2
    tm, tn = min(tm, m), min(tn, n)
    tk = min(tk, k, MM_TILE_BYTES // (tm * a.dtype.itemsize))
    assert m % tm == 0 and n % tn == 0 and k % tk == 0, (a.shape, w.shape)
    nk = k // tk
    return pl.pallas_call(
        functools.partial(_mm_kernel, act=act, nk=nk),
        grid=(m // tm, n // tn, nk),
        in_specs=[pl.BlockSpec((tm, tk), lambda i, j, kk: (i, kk)),
                  pl.BlockSpec((tk, tn), lambda i, j, kk: (kk, j))],
        out_specs=pl.BlockSpec((tm, tn), lambda i, j, kk: (i, j)),
        out_shape=jax.ShapeDtypeStruct((m, n), out_dtype),
        scratch_shapes=[pltpu.VMEM((tm, tn), F32)] if nk > 1 else [],
        compiler_params=_params("parallel", "parallel", "arbitrary"),
        name=name,
    )(a, w)


def _rms(x, g):
    return x * lax.rsqrt(jnp.mean(x * x, axis=-1, keepdims=True) + EPS) * g


def _rms_kernel(x_ref, g_ref, o_ref):
    o_ref[...] = _rms(x_ref[...], g_ref[...]).astype(o_ref.dtype)


def rms_norm(x, g, *, out_dtype=BF16, tr=256):
    m, d = x.shape
    tr = min(tr, m)
    return pl.pallas_call(
        _rms_kernel,
        grid=(m // tr,),
        in_specs=[pl.BlockSpec((tr, d), lambda i: (i, 0)), pl.BlockSpec((1, d), lambda i: (0, 0))],
        out_specs=pl.BlockSpec((tr, d), lambda i: (i, 0)),
        out_shape=jax.ShapeDtypeStruct((m, d), out_dtype),
        compiler_params=_params("parallel"),
        name="rms_norm",
    )(x, g.reshape(1, d))


def _add_rms_kernel(h_ref, u_ref, gpost_ref, *rest, with_pre):
    hn = h_ref[...] + _rms(u_ref[...].astype(F32), gpost_ref[...])
    if with_pre:
        gpre_ref, hn_ref, y_ref = rest
        y_ref[...] = _rms(hn, gpre_ref[...]).astype(y_ref.dtype)
    else:
        hn_ref, = rest
    hn_ref[...] = hn


def add_rms(h, u, g_post, g_pre=None, *, pre_dtype=BF16, tr=128):
    m, d = h.shape
    tr = min(tr, m)
    with_pre = g_pre is not None
    row = pl.BlockSpec((tr, d), lambda i: (i, 0))
    vec = pl.BlockSpec((1, d), lambda i: (0, 0))
    args = [h, u, g_post.reshape(1, d)] + ([g_pre.reshape(1, d)] if with_pre else [])
    out_shape = [jax.ShapeDtypeStruct((m, d), F32)] + ([jax.ShapeDtypeStruct((m, d), pre_dtype)] if with_pre else [])
    outs = pl.pallas_call(
        functools.partial(_add_rms_kernel, with_pre=with_pre),
        grid=(m // tr,),
        in_specs=[row, row, vec] + ([vec] if with_pre else []),
        out_specs=[row] * len(out_shape),
        out_shape=out_shape,
        compiler_params=_params("parallel"),
        name="add_rms",
    )(*args)
    return outs if with_pre else (outs[0], None)


def _mlstm_gate_kernel(x_ref, w_ref, b_ref, o_ref):
    g = jnp.dot(x_ref[...], w_ref[...], preferred_element_type=F32) + b_ref[...]
    lane = lax.broadcasted_iota(jnp.int32, g.shape, 1)
    ig = A_GATE_CAP * jnp.tanh(g / A_GATE_CAP)
    o_ref[...] = jnp.where(lane < A_HEADS, ig, _log_sigmoid(g))


def mlstm_gates(x, w_gate, b_if, *, tr=512):
    m, d = x.shape
    tr = min(tr, m)
    ng = 128
    w = jnp.zeros((d, ng), BF16).at[:, :2 * A_HEADS].set(w_gate.astype(BF16))
    b = jnp.zeros((1, ng), F32).at[0, :2 * A_HEADS].set(b_if.astype(F32))
    return pl.pallas_call(
        _mlstm_gate_kernel,
        grid=(m // tr,),
        in_specs=[pl.BlockSpec((tr, d), lambda i: (i, 0)), pl.BlockSpec((d, ng), lambda i: (0, 0)),
                  pl.BlockSpec((1, ng), lambda i: (0, 0))],
        out_specs=pl.BlockSpec((tr, ng), lambda i: (i, 0)),
        out_shape=jax.ShapeDtypeStruct((m, ng), F32),
        compiler_params=_params("parallel"),
        name="mlstm_gates",
    )(x, w, b)


def _mlstm_kernel(q_ref, k_ref, v_ref, og_ref, igc_ref, igr_ref, lfc_ref, lfr_ref, g_ref,
                  out_ref, c_ref, n_ref, m_ref, *, L, qk_scale):
    @pl.when(pl.program_id(2) == 0)
    def _():
        c_ref[...] = jnp.zeros_like(c_ref)
        n_ref[...] = jnp.zeros_like(n_ref)
        m_ref[...] = jnp.zeros_like(m_ref)

    q = q_ref[...] * qk_scale
    k = k_ref[...]
    v = v_ref[...]
    ig_col, ig_row = igc_ref[...], igr_ref[...]
    lf_col, lf_row = lfc_ref[...], lfr_ref[...]
    m_prev = m_ref[...]

    row = lax.broadcasted_iota(jnp.int32, (L, L), 0)
    col = lax.broadcasted_iota(jnp.int32, (L, L), 1)
    causal = col <= row
    tri = causal.astype(F32)
    b_col = jnp.sum(tri * lf_row, axis=1, keepdims=True)
    b_row = jnp.sum((row <= col).astype(F32) * lf_col, axis=0, keepdims=True)
    b_last = jnp.sum(lf_row, axis=1, keepdims=True)

    d = jnp.where(causal, b_col - b_row + ig_row, -jnp.inf)
    inter = b_col + m_prev
    m_t = jnp.maximum(jnp.max(d, axis=1, keepdims=True), inter)
    s = lax.dot_general(q, k, (((1,), (1,)), ((), ())), preferred_element_type=F32) * jnp.exp(d - m_t)
    inter_w = jnp.exp(inter - m_t)
    c_prev = c_ref[...]
    n_prev = n_ref[...]
    num = (jnp.dot(s.astype(BF16), v, preferred_element_type=F32)
           + inter_w * jnp.dot(q, c_prev.astype(BF16), preferred_element_type=F32))
    qn = jnp.sum(q.astype(F32) * n_prev, axis=1, keepdims=True)
    den = jnp.sum(s, axis=1, keepdims=True) + inter_w * qn
    h = num / jnp.maximum(jnp.abs(den), jnp.exp(-m_t))

    a_col = b_last - b_col + ig_col
    a_row = b_last - b_row + ig_row
    m_new = jnp.maximum(b_last + m_prev, jnp.max(a_row, axis=1, keepdims=True))
    w_col = jnp.exp(a_col - m_new)
    decay = jnp.exp(b_last + m_prev - m_new)
    kw = k.astype(F32) * w_col
    c_ref[...] = decay * c_prev + lax.dot_general(
        kw.astype(BF16), v, (((0,), (0,)), ((), ())), preferred_element_type=F32)
    n_ref[...] = decay * n_prev + jnp.sum(kw, axis=0, keepdims=True)
    m_ref[...] = m_new

    hn = h * lax.rsqrt(jnp.mean(h * h, axis=1, keepdims=True) + EPS) * g_ref[...]
    out_ref[...] = (hn * _sigmoid(og_ref[...].astype(F32))).astype(out_ref.dtype)


def mlstm_scan(proj, gates, norm_g, *, bsz, seq, L=128):
    h_ = A_HEADS
    d_model = norm_g.shape[0]
    dv = d_model // h_
    dqk = dv // 2
    nc = seq // L
    g4 = gates[:, :2 * h_].reshape(bsz, nc, L, 2 * h_).transpose(0, 3, 1, 2)
    ig, lf = g4[:, :h_], g4[:, h_:]
    col = lambda t: t[..., None]
    rowv = lambda t: t[..., None, :]
    col_spec = pl.BlockSpec((None, None, None, L, 1), lambda b, h, c: (b, h, c, 0, 0))
    row_spec = pl.BlockSpec((None, None, None, 1, L), lambda b, h, c: (b, h, c, 0, 0))
    kq = 2 * h_ * dqk // dv
    return pl.pallas_call(
        functools.partial(_mlstm_kernel, L=L, qk_scale=dqk ** -0.5),
        grid=(bsz, h_, nc),
        in_specs=[pl.BlockSpec((L, dqk), lambda b, h, c: (b * nc + c, h)),
                  pl.BlockSpec((L, dqk), lambda b, h, c: (b * nc + c, h_ + h)),
                  pl.BlockSpec((L, dv), lambda b, h, c: (b * nc + c, kq + h)),
                  pl.BlockSpec((L, dv), lambda b, h, c: (b * nc + c, kq + h_ + h)),
                  col_spec, row_spec, col_spec, row_spec,
                  pl.BlockSpec((1, dv), lambda b, h, c: (0, h))],
        out_specs=pl.BlockSpec((L, dv), lambda b, h, c: (b * nc + c, h)),
        out_shape=jax.ShapeDtypeStruct((bsz * seq, d_model), BF16),
        scratch_shapes=[pltpu.VMEM((dqk, dv), F32), pltpu.VMEM((1, dqk), F32), pltpu.VMEM((1, 1), F32)],
        compiler_params=_params("parallel", "parallel", "arbitrary"),
        name="mlstm_scan",
    )(proj, proj, proj, proj, col(ig), rowv(ig), col(lf), rowv(lf), norm_g.reshape(1, d_model).astype(F32))


def mlstm_mixer(x, w_in, b_if, norm_g, w_out, *, bsz, seq):
    d_model = x.shape[1]
    n_main = w_in.shape[1] - 2 * A_HEADS
    proj = matmul(x, w_in[:, :n_main].astype(BF16), out_dtype=BF16, name="mlstm_in")
    gates = mlstm_gates(x, w_in[:, n_main:], b_if)
    hg = mlstm_scan(proj, gates, norm_g, bsz=bsz, seq=seq)
    return matmul(hg, w_out.astype(BF16), name="mlstm_out")


def _sb_kernel(q_ref, k_ref, v_ref, o_ref, *, scale, hg):
    T = Q_BLOCK
    dh = q_ref.shape[1] // hg
    qi = pl.program_id(2)
    row = lax.broadcasted_iota(jnp.int32, (T, T), 0)
    col = lax.broadcasted_iota(jnp.int32, (T, T), 1)
    strict = col < row
    upper = (row > col).astype(BF16)
    upper2 = jnp.concatenate([upper, upper], axis=0)

    def head_block(q, k, v, carry, acc, masked):
        z = lax.dot_general(q, k, (((1,), (1,)), ((), ())), preferred_element_type=F32) * scale
        sp = jnp.log1p(jnp.exp(-jnp.abs(z)))
        ls_pos = jnp.minimum(z, 0.0) - sp
        lk = jnp.minimum(-z, 0.0) - sp
        if masked:
            lk = jnp.where(strict, lk, 0.0)
        hi = lk.astype(BF16)
        lo = (lk - hi.astype(F32)).astype(BF16)
        after = jnp.dot(jnp.concatenate([hi, lo], axis=1), upper2, preferred_element_type=F32)
        w = jnp.exp(ls_pos + after + carry)
        if masked:
            w = jnp.where(strict, w, 0.0)
        acc = acc + jnp.dot(w.astype(BF16), v, preferred_element_type=F32)
        carry = carry + jnp.sum(lk, axis=1, keepdims=True)
        return carry, acc

    heads = [slice(h * dh, (h + 1) * dh) for h in range(hg)]
    qs = [q_ref[:, sl] for sl in heads]

    def block(kb, carries, accs, masked):
        start = pl.multiple_of(kb * T, T)
        out = [head_block(qs[h], k_ref[pl.ds(start, T), heads[h]], v_ref[pl.ds(start, T), heads[h]],
                          carries[h], accs[h], masked) for h in range(hg)]
        return tuple(o[0] for o in out), tuple(o[1] for o in out)

    carries, accs = block(qi, (jnp.zeros((T, 1), F32),) * hg, (jnp.zeros((T, dh), F32),) * hg, True)

    def live(carries):
        worst = functools.reduce(jnp.maximum, carries)
        return jnp.max(worst) >= SB_UNDERFLOW

    def cond(state):
        j, alive, _, _ = state
        return jnp.logical_and(j < qi, alive)

    def body(state):
        j, _, carries, accs = state
        carries, accs = block(qi - 1 - j, carries, accs, False)
        return j + 1, live(carries), carries, accs

    _, _, _, accs = lax.while_loop(cond, body, (jnp.int32(0), live(carries), carries, accs))
    for h in range(hg):
        o_ref[:, heads[h]] = accs[h].astype(o_ref.dtype)


def stick_breaking_scan(qkv, *, bsz, seq, hg=4):
    d_model = qkv.shape[1] // 3
    dh = d_model // B_HEADS
    nq = seq // Q_BLOCK
    ng = B_HEADS // hg
    wide = hg * dh
    return pl.pallas_call(
        functools.partial(_sb_kernel, scale=dh ** -0.5, hg=hg),
        grid=(bsz, ng, nq),
        in_specs=[pl.BlockSpec((Q_BLOCK, wide), lambda b, h, i: (b * nq + i, h)),
                  pl.BlockSpec((seq, wide), lambda b, h, i: (b, ng + h)),
                  pl.BlockSpec((seq, wide), lambda b, h, i: (b, 2 * ng + h))],
        out_specs=pl.BlockSpec((Q_BLOCK, wide), lambda b, h, i: (b * nq + i, h)),
        out_shape=jax.ShapeDtypeStruct((bsz * seq, d_model), BF16),
        compiler_params=_params("parallel", "parallel", "arbitrary"),
        name="stick_breaking_scan",
    )(qkv, qkv, qkv)


def stick_breaking_mixer(x, w_in, w_out, *, bsz, seq):
    qkv = matmul(x, w_in.astype(BF16), out_dtype=BF16, name="sb_in")
    o = stick_breaking_scan(qkv, bsz=bsz, seq=seq)
    return matmul(o, w_out.astype(BF16), name="sb_out")


def _shift_kernel(x_ref, prev_ref, mu_ref, *o_refs, tiles_per_seq):
    x = x_ref[...]
    first = pl.program_id(0) % tiles_per_seq == 0
    prev_last = jnp.where(first, 0.0, prev_ref[7:8, :])
    row = lax.broadcasted_iota(jnp.int32, x.shape, 0)
    x_prev = jnp.where(row == 0, prev_last, pltpu.roll(x, 1, 0))
    xx = x_prev - x
    for i, o_ref in enumerate(o_refs):
        o_ref[...] = (x + xx * mu_ref[i:i + 1, :]).astype(o_ref.dtype)


def token_shift_mix(x, mu, *, seq, tr=256):
    m, d = x.shape
    tr = min(tr, seq)
    n_out = mu.shape[0]
    row = pl.BlockSpec((tr, d), lambda i: (i, 0))
    return pl.pallas_call(
        functools.partial(_shift_kernel, tiles_per_seq=seq // tr),
        grid=(m // tr,),
        in_specs=[row, pl.BlockSpec((8, d), lambda i: (jnp.maximum(i * (tr // 8) - 1, 0), 0)),
                  pl.BlockSpec((n_out, d), lambda i: (0, 0))],
        out_specs=[row] * n_out,
        out_shape=[jax.ShapeDtypeStruct((m, d), BF16)] * n_out,
        compiler_params=_params("parallel"),
        name="token_shift_mix",
    )(x, x, mu)


def _split_bf16(x, terms):
    parts = []
    for _ in range(terms - 1):
        p = x.astype(BF16)
        parts.append(p)
        x = x - p.astype(F32)
    parts.append(x.astype(BF16))
    return parts


def _rwkv_kernel(r_ref, k_ref, v_ref, wl_ref, al_ref, g_ref, w0_ref, a0_ref, kkw_ref, ka_ref, rk_ref,
                 gng_ref, gnb_ref, o_ref, s_ref, skt_ref, rt_ref, nb_ref, ypart_ref, mbk_ref, bk_ref,
                 gam_ref, bon_ref, *, L, G, SG):
    N = C_HEAD_DIM
    P = 2 * N
    assert L == N and P == 128

    @pl.when(pl.program_id(2) == 0)
    def _():
        s_ref[...] = jnp.zeros_like(s_ref)

    r128 = lax.broadcasted_iota(jnp.int32, (P, P), 0)
    c128 = lax.broadcasted_iota(jnp.int32, (P, P), 1)
    same_head = (r128 // N) == (c128 // N)
    ones_blk = same_head.astype(BF16)
    ones2 = jnp.concatenate([ones_blk, ones_blk], axis=0)
    rl = lax.broadcasted_iota(jnp.int32, (L, P), 0)
    cl_ = lax.broadcasted_iota(jnp.int32, (L, P), 1) % N
    strict = cl_ < rl
    incl = cl_ <= rl
    tri = (lax.broadcasted_iota(jnp.int32, (L, L), 1) <= lax.broadcasted_iota(jnp.int32, (L, L), 0)).astype(BF16)
    lane_head0 = lax.broadcasted_iota(jnp.int32, (1, P), 1) < N

    def seg_sum(x, terms=2):
        if terms == 1:
            return jnp.dot(x.astype(BF16), ones_blk, preferred_element_type=F32)
        hi, lo = _split_bf16(x, 2)
        return jnp.dot(jnp.concatenate([hi, lo], axis=1), ones2, preferred_element_type=F32)

    def lanes(g):
        return pl.ds(g * P, P) if isinstance(g, int) else pl.ds(pl.multiple_of(g * P, P), P)

    def for_each_pair(fn):
        if G == SG:
            for g in range(G):
                fn(g)
            return

        def sub_group(sg, carry):
            for g in range(SG):
                fn(sg * SG + g)
            return carry

        lax.fori_loop(0, G // SG, sub_group, 0)

    def prepare(g):
        sl = lanes(g)
        r = r_ref[:, sl]
        k = k_ref[:, sl]
        v = v_ref[:, sl]
        wraw = wl_ref[:, sl] + w0_ref[:, sl]
        w_log = -(jnp.maximum(-wraw, 0.0) + jnp.log1p(jnp.exp(-jnp.abs(wraw)))) - 0.5
        ld = -jnp.exp(w_log)
        a = _sigmoid(al_ref[:, sl] + a0_ref[:, sl])
        kkr = k * kkw_ref[:, sl]
        kk = kkr * lax.rsqrt(jnp.maximum(seg_sum(kkr * kkr), 1e-24))
        kh = k * (1.0 + (a - 1.0) * ka_ref[:, sl])
        b = -(kk * a)
        bon_ref[g] = seg_sum(r * kh * rk_ref[:, sl])

        cl3 = jnp.dot(tri, jnp.concatenate(_split_bf16(ld, 3), axis=1), preferred_element_type=F32)
        cl = cl3[:, :P] + cl3[:, P:2 * P] + cl3[:, 2 * P:]
        cl_last = cl[L - 1:L, :]
        e_inv = jnp.exp(-cl)
        e_tail = jnp.exp(cl_last - cl)
        kkt = kk * jnp.exp(cl - ld)
        rt = r * jnp.exp(cl)
        lhs = jnp.concatenate([kkt, rt], axis=0)
        rhs = jnp.concatenate([b * e_inv, kh * e_inv], axis=0).astype(BF16)
        nt = (((1,), (1,)), ((), ()))
        res0 = lax.dot_general(jnp.where(lane_head0, lhs, 0.0).astype(BF16), rhs, nt, preferred_element_type=F32)
        res1 = lax.dot_general(jnp.where(lane_head0, 0.0, lhs).astype(BF16), rhs, nt, preferred_element_type=F32)
        pair = lambda blk_r, blk_c: jnp.concatenate(
            [res0[blk_r * L:(blk_r + 1) * L, blk_c * L:(blk_c + 1) * L],
             res1[blk_r * L:(blk_r + 1) * L, blk_c * L:(blk_c + 1) * L]], axis=1)
        nb = jnp.where(strict, pair(0, 0), 0.0)
        nk = jnp.where(strict, pair(0, 1), 0.0)
        mb = jnp.where(incl, pair(1, 0), 0.0)
        mk = jnp.where(incl, pair(1, 1), 0.0)

        s_blk = s_ref[g]
        ls = lax.dot_general(lhs.astype(BF16), s_blk.astype(BF16), nt, preferred_element_type=F32)
        v_blk = jnp.where(same_head, jnp.concatenate([v, v], axis=0), 0.0).astype(BF16)
        rhs_rows = ls[:L] + jnp.dot(nk.astype(BF16), v_blk, preferred_element_type=F32)
        rhs_t = rhs_rows.T
        rt_ref[g] = jnp.concatenate([rhs_t[:N], rhs_t[N:]], axis=1)
        nb_ref[g] = nb
        skt_ref[g] = jnp.zeros((N, P), F32)
        ypart_ref[g] = ls[L:]
        mbk_ref[g] = jnp.concatenate([mb, mk], axis=1).astype(BF16)
        bk_ref[g] = jnp.concatenate([b * e_tail, kh * e_tail], axis=0).astype(BF16)
        gam_ref[g] = jnp.exp(cl_last)

    for_each_pair(prepare)

    lane_s = lax.broadcasted_iota(jnp.int32, (N, P), 1) % N

    def step(t, carry):
        at_t = lane_s == t
        for g in range(G):
            skt = skt_ref[g]
            red = seg_sum(skt * nb_ref[g, pl.ds(t, 1), :], terms=1)
            skt_ref[g] = jnp.where(at_t, rt_ref[g] + red, skt)
        return carry

    lax.fori_loop(0, L, step, 0)

    def finish(g):
        sl = lanes(g)
        v = v_ref[:, sl]
        skt_t = skt_ref[g].T
        sk_blk = jnp.where(same_head, jnp.concatenate([skt_t, skt_t], axis=1), 0.0).astype(BF16)
        sk_rows = jnp.concatenate([skt_t[:N], skt_t[N:]], axis=1)
        v_blk = jnp.where(same_head, jnp.concatenate([v, v], axis=0), 0.0).astype(BF16)
        y = ypart_ref[g] + jnp.dot(mbk_ref[g], jnp.concatenate([sk_blk, v_blk], axis=0),
                                   preferred_element_type=F32)
        d_s = lax.dot_general(jnp.concatenate([sk_rows, v], axis=0).astype(BF16), bk_ref[g],
                              (((0,), (0,)), ((), ())), preferred_element_type=F32)
        s_ref[g] = gam_ref[g] * s_ref[g] + jnp.where(same_head, d_s, 0.0)

        mean = seg_sum(y) * (1.0 / N)
        yc = y - mean
        var = seg_sum(yc * yc) * (1.0 / N)
        yn = yc * lax.rsqrt(var + C_GN_EPS) * gng_ref[:, sl] + gnb_ref[:, sl]
        o_ref[:, sl] = ((yn + bon_ref[g] * v) * g_ref[:, sl]).astype(o_ref.dtype)

    for_each_pair(finish)


def rwkv7_scan(r, k, v, wl, al, g, w0, a0, k_k, k_a, r_k, gn_g, gn_b, *, bsz, seq, G=32, SG=8):
    m, d = r.shape
    L = C_HEAD_DIM
    P = 2 * C_HEAD_DIM
    nc = seq // L
    wide = G * P
    tile = pl.BlockSpec((L, wide), lambda b, j, c: (b * nc + c, j))
    vec = pl.BlockSpec((1, wide), lambda b, j, c: (0, j))
    vecs = [t.reshape(1, d).astype(F32) for t in (w0, a0, k_k, k_a, r_k, gn_g, gn_b)]
    return pl.pallas_call(
        functools.partial(_rwkv_kernel, L=L, G=G, SG=SG),
        grid=(bsz, d // wide, nc),
        in_specs=[tile] * 6 + [vec] * 7,
        out_specs=tile,
        out_shape=jax.ShapeDtypeStruct((m, d), BF16),
        scratch_shapes=[pltpu.VMEM((G, P, P), F32),
                        pltpu.VMEM((G, C_HEAD_DIM, P), F32),
                        pltpu.VMEM((G, C_HEAD_DIM, P), F32),
                        pltpu.VMEM((G, L, P), F32),
                        pltpu.VMEM((G, L, P), F32),
                        pltpu.VMEM((G, L, 2 * P), BF16),
                        pltpu.VMEM((G, 2 * L, P), BF16),
                        pltpu.VMEM((G, 1, P), F32),
                        pltpu.VMEM((G, L, P), F32)],
        compiler_params=_params("parallel", "parallel", "arbitrary"),
        name="rwkv7_scan",
    )(r, k, v, wl, al, g, *vecs)


def rwkv7_mixer(x, mu, w_rkv, w0, w1, w2, a0, a1, a2, g1, g2, k_k, k_a, r_k, gn_g, gn_b, w_out, *, bsz, seq):
    xr, xw, xk, xv, xa, xg = token_shift_mix(x, mu, seq=seq)
    r = matmul(xr, w_rkv[0].astype(BF16), name="rwkv_r")
    k = matmul(xk, w_rkv[1].astype(BF16), name="rwkv_k")
    v = matmul(xv, w_rkv[2].astype(BF16), name="rwkv_v")
    wl = matmul(matmul(xw, w1.astype(BF16), act="tanh", out_dtype=BF16, name="rwkv_w1"), w2.astype(BF16),
                name="rwkv_w2")
    al = matmul(matmul(xa, a1.astype(BF16), out_dtype=BF16, name="rwkv_a1"), a2.astype(BF16), name="rwkv_a2")
    n_g = g1.shape[1]
    n_gp = -(-n_g // 128) * 128
    g1p = jnp.zeros((g1.shape[0], n_gp), BF16).at[:, :n_g].set(g1.astype(BF16))
    g2p = jnp.zeros((n_gp, g2.shape[1]), BF16).at[:n_g, :].set(g2.astype(BF16))
    g = matmul(matmul(xg, g1p, act="sigmoid", out_dtype=BF16, name="rwkv_g1"), g2p, name="rwkv_g2")
    y = rwkv7_scan(r, k, v, wl, al, g, w0, a0, k_k, k_a, r_k, gn_g, gn_b, bsz=bsz, seq=seq)
    return matmul(y, w_out.astype(BF16), name="rwkv_out")


def kernel(x, norm_g, mlp_w1, mlp_w2, a_w_in, a_b_if, a_norm_g, a_w_out, b_w_in, b_w_out, c_mu, c_w_rkv,
           c_w0, c_w1, c_w2, c_a0, c_a1, c_a2, c_g1, c_g2, c_k_k, c_k_a, c_r_k, c_gn_g, c_gn_b, c_w_out):
    bsz, seq, d = x.shape
    depth = norm_g.shape[0]
    h = x.reshape(bsz * seq, d)
    kinds = [layer % 3 for layer in range(depth)]
    pre_dtype = lambda kind: F32 if kind == 2 else BF16
    y = rms_norm(h, norm_g[0, 0], out_dtype=pre_dtype(kinds[0]))
    for layer in range(depth):
        kind, idx = kinds[layer], layer // 3
        if kind == 0:
            u = mlstm_mixer(y, a_w_in[idx], a_b_if[idx], a_norm_g[idx], a_w_out[idx], bsz=bsz, seq=seq)
        elif kind == 1:
            u = stick_breaking_mixer(y, b_w_in[idx], b_w_out[idx], bsz=bsz, seq=seq)
        else:
            u = rwkv7_mixer(y, c_mu[idx], c_w_rkv[idx], c_w0[idx], c_w1[idx], c_w2[idx], c_a0[idx], c_a1[idx],
                            c_a2[idx], c_g1[idx], c_g2[idx], c_k_k[idx], c_k_a[idx], c_r_k[idx], c_gn_g[idx],
                            c_gn_b[idx], c_w_out[idx], bsz=bsz, seq=seq)
        h, y = add_rms(h, u, norm_g[layer, 1], norm_g[layer, 2])
        mid = matmul(y, mlp_w1[layer].astype(BF16), act="relu2", out_dtype=BF16, name="mlp_up")
        u = matmul(mid, mlp_w2[layer].astype(BF16), name="mlp_down")
        last = layer == depth - 1
        h, y = add_rms(h, u, norm_g[layer, 3], None if last else norm_g[layer + 1, 0],
                       pre_dtype=BF16 if last else pre_dtype(kinds[layer + 1]))
    return h.reshape(bsz, seq, d)
```

```python
import functools

import jax
import jax.numpy as jnp
from jax import lax
from jax.experimental import pallas as pl
from jax.experimental.pallas import tpu as pltpu

F32 = jnp.float32
BF16 = jnp.bfloat16

EPS = 1e-6
VMEM_LIMIT_BYTES = 56 * 1024 * 1024
MM_TILE_BYTES = 8 * 1024 * 1024

A_HEADS = 8
A_GATE_CAP = 15.0
B_HEADS = 32
Q_BLOCK = 128
C_HEAD_DIM = 64
C_GN_EPS = 64e-5
SB_UNDERFLOW = -104.0


def _params(*sem):
    return pltpu.CompilerParams(dimension_semantics=sem, vmem_limit_bytes=VMEM_LIMIT_BYTES)


def _softplus_neg_abs(z):
    return jnp.log(1.0 + jnp.exp(-jnp.abs(z)))


def _log_sigmoid(z):
    return jnp.minimum(z, 0.0) - _softplus_neg_abs(z)


def _sigmoid(z):
    return 1.0 / (1.0 + jnp.exp(-z))


def _round_robin(gens):
    results = [None] * len(gens)
    active = list(enumerate(gens))
    while active:
        still = []
        for i, gen in active:
            try:
                next(gen)
                still.append((i, gen))
            except StopIteration as stop:
                results[i] = stop.value
        active = still
    return results


_ACTS = {
    "none": lambda r: r,
    "relu2": lambda r: jnp.square(jnp.maximum(r, 0.0)),
    "sigmoid": _sigmoid,
    "tanh": jnp.tanh,
}


def _mm_kernel(a_ref, w_ref, o_ref, *scratch, act, nk):
    part = jnp.dot(a_ref[...].astype(BF16), w_ref[...].astype(BF16), preferred_element_type=F32)
    if nk == 1:
        o_ref[...] = _ACTS[act](part).astype(o_ref.dtype)
        return
    acc_ref, = scratch
    k = pl.program_id(2)

    @pl.when(k == 0)
    def _():
        acc_ref[...] = part

    @pl.when(k > 0)
    def _():
        acc_ref[...] += part

    @pl.when(k == nk - 1)
    def _():
        o_ref[...] = _ACTS[act](acc_ref[...]).astype(o_ref.dtype)


def matmul(a, w, *, n=None, act="none", out_dtype=F32, tm=1024, tn=1024, tk=4096, name="matmul"):
    m, k = a.shape
    k2, n_w = w.shape
    n = n_w if n is None else n
    assert k == k2 and n <= n_w
    tm, tn = min(tm, m), min(tn, n)
    tk = min(tk, k, MM_TILE_BYTES // (tm * a.dtype.itemsize))
    assert m % tm == 0 and n % tn == 0 and k % tk == 0, (a.shape, w.shape)
    nk = k // tk
    return pl.pallas_call(
        functools.partial(_mm_kernel, act=act, nk=nk),
        grid=(m // tm, n // tn, nk),
        in_specs=[pl.BlockSpec((tm, tk), lambda i, j, kk: (i, kk)),
                  pl.BlockSpec((tk, tn), lambda i, j, kk: (kk, j))],
        out_specs=pl.BlockSpec((tm, tn), lambda i, j, kk: (i, j)),
        out_shape=jax.ShapeDtypeStruct((m, n), out_dtype),
        scratch_shapes=[pltpu.VMEM((tm, tn), F32)] if nk > 1 else [],
        compiler_params=_params("parallel", "parallel", "arbitrary"),
        name=name,
    )(a, w)


def _rms(x, g):
    return x * lax.rsqrt(jnp.mean(x * x, axis=-1, keepdims=True) + EPS) * g


def _rms_kernel(x_ref, g_ref, o_ref):
    o_ref[...] = _rms(x_ref[...], g_ref[...]).astype(o_ref.dtype)


def rms_norm(x, g, *, out_dtype=BF16, tr=256):
    m, d = x.shape
    tr = min(tr, m)
    return pl.pallas_call(
        _rms_kernel,
        grid=(m // tr,),
        in_specs=[pl.BlockSpec((tr, d), lambda i: (i, 0)), pl.BlockSpec((1, d), lambda i: (0, 0))],
        out_specs=pl.BlockSpec((tr, d), lambda i: (i, 0)),
        out_shape=jax.ShapeDtypeStruct((m, d), out_dtype),
        compiler_params=_params("parallel"),
        name="rms_norm",
    )(x, g.reshape(1, d))


def _add_rms_kernel(h_ref, u_ref, gpost_ref, *rest, with_pre):
    hn = h_ref[...] + _rms(u_ref[...].astype(F32), gpost_ref[...])
    if with_pre:
        gpre_ref, hn_ref, y_ref = rest
        y_ref[...] = _rms(hn, gpre_ref[...]).astype(y_ref.dtype)
    else:
        hn_ref, = rest
    hn_ref[...] = hn


def add_rms(h, u, g_post, g_pre=None, *, pre_dtype=BF16, tr=128):
    m, d = h.shape
    tr = min(tr, m)
    with_pre = g_pre is not None
    row = pl.BlockSpec((tr, d), lambda i: (i, 0))
    vec = pl.BlockSpec((1, d), lambda i: (0, 0))
    args = [h, u, g_post.reshape(1, d)] + ([g_pre.reshape(1, d)] if with_pre else [])
    out_shape = [jax.ShapeDtypeStruct((m, d), F32)] + ([jax.ShapeDtypeStruct((m, d), pre_dtype)] if with_pre else [])
    outs = pl.pallas_call(
        functools.partial(_add_rms_kernel, with_pre=with_pre),
        grid=(m // tr,),
        in_specs=[row, row, vec] + ([vec] if with_pre else []),
        out_specs=[row] * len(out_shape),
        out_shape=out_shape,
        compiler_params=_params("parallel"),
        name="add_rms",
    )(*args)
    return outs if with_pre else (outs[0], None)


def _mlstm_gate_kernel(x_ref, w_ref, b_ref, o_ref):
    g = jnp.dot(x_ref[...], w_ref[...], preferred_element_type=F32) + b_ref[...]
    lane = lax.broadcasted_iota(jnp.int32, g.shape, 1)
    ig = A_GATE_CAP * jnp.tanh(g / A_GATE_CAP)
    o_ref[...] = jnp.where(lane < A_HEADS, ig, _log_sigmoid(g))


def mlstm_gates(x, w_gate, b_if, *, tr=512):
    m, d = x.shape
    tr = min(tr, m)
    ng = 128
    w = jnp.zeros((d, ng), BF16).at[:, :2 * A_HEADS].set(w_gate.astype(BF16))
    b = jnp.zeros((1, ng), F32).at[0, :2 * A_HEADS].set(b_if.astype(F32))
    return pl.pallas_call(
        _mlstm_gate_kernel,
        grid=(m // tr,),
        in_specs=[pl.BlockSpec((tr, d), lambda i: (i, 0)), pl.BlockSpec((d, ng), lambda i: (0, 0)),
                  pl.BlockSpec((1, ng), lambda i: (0, 0))],
        out_specs=pl.BlockSpec((tr, ng), lambda i: (i, 0)),
        out_shape=jax.ShapeDtypeStruct((m, ng), F32),
        compiler_params=_params("parallel"),
        name="mlstm_gates",
    )(x, w, b)


def _mlstm_kernel(q_ref, k_ref, v_ref, og_ref, igc_ref, igr_ref, lfc_ref, lfr_ref, g_ref,
                  out_ref, c_ref, n_ref, m_ref, *, L, hg, qk_scale):
    @pl.when(pl.program_id(2) == 0)
    def _():
        c_ref[...] = jnp.zeros_like(c_ref)
        n_ref[...] = jnp.zeros_like(n_ref)
        m_ref[...] = jnp.zeros_like(m_ref)

    dqk = q_ref.shape[1] // hg
    dv = v_ref.shape[1] // hg
    row = lax.broadcasted_iota(jnp.int32, (L, L), 0)
    col = lax.broadcasted_iota(jnp.int32, (L, L), 1)
    causal = col <= row
    tri = causal.astype(F32)
    tri_t = (row <= col).astype(F32)

    def head(h):
        qk_cols = slice(h * dqk, (h + 1) * dqk)
        v_cols = slice(h * dv, (h + 1) * dv)
        q = q_ref[:, qk_cols] * qk_scale
        k = k_ref[:, qk_cols]
        v = v_ref[:, v_cols]
        ig_col, ig_row = igc_ref[h], igr_ref[h]
        lf_col, lf_row = lfc_ref[h], lfr_ref[h]
        m_prev = m_ref[h]

        b_col = jnp.sum(tri * lf_row, axis=1, keepdims=True)
        b_row = jnp.sum(tri_t * lf_col, axis=0, keepdims=True)
        b_last = jnp.sum(lf_row, axis=1, keepdims=True)

        d = jnp.where(causal, b_col - b_row + ig_row, -jnp.inf)
        inter = b_col + m_prev
        m_t = jnp.maximum(jnp.max(d, axis=1, keepdims=True), inter)
        qk = lax.dot_general(q, k, (((1,), (1,)), ((), ())), preferred_element_type=F32)
        yield
        s = qk * jnp.exp(d - m_t)
        inter_w = jnp.exp(inter - m_t)
        c_prev = c_ref[h]
        n_prev = n_ref[h]
        sv = jnp.dot(s.astype(BF16), v, preferred_element_type=F32)
        yield
        qc = jnp.dot(q, c_prev.astype(BF16), preferred_element_type=F32)
        yield
        num = sv + inter_w * qc
        qn = jnp.sum(q.astype(F32) * n_prev, axis=1, keepdims=True)
        den = jnp.sum(s, axis=1, keepdims=True) + inter_w * qn
        hid = num / jnp.maximum(jnp.abs(den), jnp.exp(-m_t))

        a_col = b_last - b_col + ig_col
        a_row = b_last - b_row + ig_row
        m_new = jnp.maximum(b_last + m_prev, jnp.max(a_row, axis=1, keepdims=True))
        w_col = jnp.exp(a_col - m_new)
        decay = jnp.exp(b_last + m_prev - m_new)
        kw = k.astype(F32) * w_col
        kv = lax.dot_general(kw.astype(BF16), v, (((0,), (0,)), ((), ())), preferred_element_type=F32)
        yield
        c_new = decay * c_prev + kv
        n_new = decay * n_prev + jnp.sum(kw, axis=0, keepdims=True)

        hn = hid * lax.rsqrt(jnp.mean(hid * hid, axis=1, keepdims=True) + EPS) * g_ref[:, v_cols]
        out = (hn * _sigmoid(og_ref[:, v_cols].astype(F32))).astype(out_ref.dtype)
        return c_new, n_new, m_new, out

    for h, (c_new, n_new, m_new, out) in enumerate(_round_robin([head(h) for h in range(hg)])):
        c_ref[h] = c_new
        n_ref[h] = n_new
        m_ref[h] = m_new
        out_ref[:, h * dv:(h + 1) * dv] = out


def mlstm_scan(proj, gates, norm_g, *, bsz, seq, L=128, hg=4):
    h_ = A_HEADS
    d_model = norm_g.shape[0]
    dv = d_model // h_
    dqk = dv // 2
    nc = seq // L
    ng = h_ // hg
    g4 = gates[:, :2 * h_].reshape(bsz, nc, L, 2 * h_).transpose(0, 3, 1, 2)
    ig, lf = g4[:, :h_], g4[:, h_:]
    col = lambda t: t[..., None]
    rowv = lambda t: t[..., None, :]
    col_spec = pl.BlockSpec((None, hg, None, L, 1), lambda b, h, c: (b, h, c, 0, 0))
    row_spec = pl.BlockSpec((None, hg, None, 1, L), lambda b, h, c: (b, h, c, 0, 0))
    kq = 2 * h_ * dqk // dv
    return pl.pallas_call(
        functools.partial(_mlstm_kernel, L=L, hg=hg, qk_scale=dqk ** -0.5),
        grid=(bsz, ng, nc),
        in_specs=[pl.BlockSpec((L, hg * dqk), lambda b, h, c: (b * nc + c, h)),
                  pl.BlockSpec((L, hg * dqk), lambda b, h, c: (b * nc + c, ng + h)),
                  pl.BlockSpec((L, hg * dv), lambda b, h, c: (b * nc + c, kq // hg + h)),
                  pl.BlockSpec((L, hg * dv), lambda b, h, c: (b * nc + c, (kq + h_) // hg + h)),
                  col_spec, row_spec, col_spec, row_spec,
                  pl.BlockSpec((1, hg * dv), lambda b, h, c: (0, h))],
        out_specs=pl.BlockSpec((L, hg * dv), lambda b, h, c: (b * nc + c, h)),
        out_shape=jax.ShapeDtypeStruct((bsz * seq, d_model), BF16),
        scratch_shapes=[pltpu.VMEM((hg, dqk, dv), F32), pltpu.VMEM((hg, 1, dqk), F32),
                        pltpu.VMEM((hg, 1, 1), F32)],
        compiler_params=_params("parallel", "parallel", "arbitrary"),
        name="mlstm_scan",
    )(proj, proj, proj, proj, col(ig), rowv(ig), col(lf), rowv(lf), norm_g.reshape(1, d_model).astype(F32))


def mlstm_mixer(x, w_in, b_if, norm_g, w_out, *, bsz, seq):
    n_main = w_in.shape[1] - 2 * A_HEADS
    proj = matmul(x, w_in.astype(BF16), n=n_main, out_dtype=BF16, name="mlstm_in")
    gates = mlstm_gates(x, w_in[:, n_main:], b_if)
    hg = mlstm_scan(proj, gates, norm_g, bsz=bsz, seq=seq)
    return matmul(hg, w_out.astype(BF16), name="mlstm_out")


def _sb_kernel(q_ref, k_ref, v_ref, o_ref, *, scale, hg):
    T = Q_BLOCK
    dh = q_ref.shape[1] // hg
    qi = pl.program_id(2)
    row = lax.broadcasted_iota(jnp.int32, (T, T), 0)
    col = lax.broadcasted_iota(jnp.int32, (T, T), 1)
    strict = col < row
    upper = (row > col).astype(BF16)
    upper2 = jnp.concatenate([upper, upper], axis=0)

    def head_block(q, k, v, carry, acc, masked):
        z = lax.dot_general(q, k, (((1,), (1,)), ((), ())), preferred_element_type=F32) * scale
        yield
        sp = _softplus_neg_abs(z)
        ls_pos = jnp.minimum(z, 0.0) - sp
        lk = jnp.minimum(-z, 0.0) - sp
        if masked:
            lk = jnp.where(strict, lk, 0.0)
        hi = lk.astype(BF16)
        lo = (lk - hi.astype(F32)).astype(BF16)
        after = jnp.dot(jnp.concatenate([hi, lo], axis=1), upper2, preferred_element_type=F32)
        yield
        w = jnp.exp(ls_pos + after + carry)
        if masked:
            w = jnp.where(strict, w, 0.0)
        acc = acc + jnp.dot(w.astype(BF16), v, preferred_element_type=F32)
        carry = carry + jnp.sum(lk, axis=1, keepdims=True)
        return carry, acc

    heads = [slice(h * dh, (h + 1) * dh) for h in range(hg)]
    qs = [q_ref[:, sl] for sl in heads]

    def block(kb, carries, accs, masked):
        start = pl.multiple_of(kb * T, T)
        out = _round_robin([head_block(qs[h], k_ref[pl.ds(start, T), heads[h]], v_ref[pl.ds(start, T), heads[h]],
                                       carries[h], accs[h], masked) for h in range(hg)])
        return tuple(o[0] for o in out), tuple(o[1] for o in out)

    carries, accs = block(qi, (jnp.zeros((T, 1), F32),) * hg, (jnp.zeros((T, dh), F32),) * hg, True)

    def live(carries):
        worst = functools.reduce(jnp.maximum, carries)
        return jnp.max(worst) >= SB_UNDERFLOW

    def cond(state):
        j, alive, _, _ = state
        return jnp.logical_and(j < qi, alive)

    def body(state):
        j, _, carries, accs = state
        carries, accs = block(qi - 1 - j, carries, accs, False)
        return j + 1, live(carries), carries, accs

    _, _, _, accs = lax.while_loop(cond, body, (jnp.int32(0), live(carries), carries, accs))
    for h in range(hg):
        o_ref[:, heads[h]] = accs[h].astype(o_ref.dtype)


def stick_breaking_scan(qkv, *, bsz, seq, hg=8):
    d_model = qkv.shape[1] // 3
    dh = d_model // B_HEADS
    nq = seq // Q_BLOCK
    ng = B_HEADS // hg
    wide = hg * dh
    return pl.pallas_call(
        functools.partial(_sb_kernel, scale=dh ** -0.5, hg=hg),
        grid=(bsz, ng, nq),
        in_specs=[pl.BlockSpec((Q_BLOCK, wide), lambda b, h, i: (b * nq + i, h)),
                  pl.BlockSpec((seq, wide), lambda b, h, i: (b, ng + h)),
                  pl.BlockSpec((seq, wide), lambda b, h, i: (b, 2 * ng + h))],
        out_specs=pl.BlockSpec((Q_BLOCK, wide), lambda b, h, i: (b * nq + i, h)),
        out_shape=jax.ShapeDtypeStruct((bsz * seq, d_model), BF16),
        compiler_params=_params("parallel", "parallel", "arbitrary"),
        name="stick_breaking_scan",
    )(qkv, qkv, qkv)


def stick_breaking_mixer(x, w_in, w_out, *, bsz, seq):
    qkv = matmul(x, w_in.astype(BF16), out_dtype=BF16, name="sb_in")
    o = stick_breaking_scan(qkv, bsz=bsz, seq=seq)
    return matmul(o, w_out.astype(BF16), name="sb_out")


def _shift_kernel(x_ref, prev_ref, mu_ref, *o_refs, tiles_per_seq):
    x = x_ref[...]
    first = pl.program_id(0) % tiles_per_seq == 0
    prev_last = jnp.where(first, 0.0, prev_ref[7:8, :])
    row = lax.broadcasted_iota(jnp.int32, x.shape, 0)
    x_prev = jnp.where(row == 0, prev_last, pltpu.roll(x, 1, 0))
    xx = x_prev - x
    for i, o_ref in enumerate(o_refs):
        o_ref[...] = (x + xx * mu_ref[i:i + 1, :]).astype(o_ref.dtype)


def token_shift_mix(x, mu, *, seq, tr=256):
    m, d = x.shape
    tr = min(tr, seq)
    n_out = mu.shape[0]
    row = pl.BlockSpec((tr, d), lambda i: (i, 0))
    return pl.pallas_call(
        functools.partial(_shift_kernel, tiles_per_seq=seq // tr),
        grid=(m // tr,),
        in_specs=[row, pl.BlockSpec((8, d), lambda i: (jnp.maximum(i * (tr // 8) - 1, 0), 0)),
                  pl.BlockSpec((n_out, d), lambda i: (0, 0))],
        out_specs=[row] * n_out,
        out_shape=[jax.ShapeDtypeStruct((m, d), BF16)] * n_out,
        compiler_params=_params("parallel"),
        name="token_shift_mix",
    )(x, x, mu)


def _split_bf16(x, terms):
    parts = []
    for _ in range(terms - 1):
        p = x.astype(BF16)
        parts.append(p)
        x = x - p.astype(F32)
    parts.append(x.astype(BF16))
    return parts


def _rwkv_kernel(r_ref, k_ref, v_ref, wl_ref, al_ref, g_ref, w0_ref, a0_ref, kkw_ref, ka_ref, rk_ref,
                 gng_ref, gnb_ref, o_ref, s_ref, *, L, G, SG):
    N = C_HEAD_DIM
    P = 2 * N
    assert L == N and P == 128

    @pl.when(pl.program_id(2) == 0)
    def _():
        s_ref[...] = jnp.zeros_like(s_ref)

    r128 = lax.broadcasted_iota(jnp.int32, (P, P), 0)
    c128 = lax.broadcasted_iota(jnp.int32, (P, P), 1)
    same_head = (r128 // N) == (c128 // N)
    ones_blk = same_head.astype(BF16)
    t_i = lax.broadcasted_iota(jnp.int32, (L, P), 0)
    s_i = lax.broadcasted_iota(jnp.int32, (L, P), 1) % N
    strict = s_i < t_i
    incl = s_i <= t_i
    eye = (s_i == t_i).astype(F32)
    same_block = lambda width: (t_i // width) == (s_i // width)
    base = 8
    diag_base = same_block(base)
    merges = []
    width = base
    while width < L:
        merges.append(jnp.logical_and(same_block(2 * width), jnp.logical_not(same_block(width))))
        width *= 2
    tri = (lax.broadcasted_iota(jnp.int32, (L, L), 1) <= lax.broadcasted_iota(jnp.int32, (L, L), 0)).astype(BF16)
    lane_head0 = lax.broadcasted_iota(jnp.int32, (1, P), 1) < N
    nt = (((1,), (1,)), ((), ()))

    def seg_sum(x):
        return jnp.dot(x.astype(BF16), ones_blk, preferred_element_type=F32)

    def block_diag(x_cat):
        return jnp.where(same_head, jnp.concatenate([x_cat, x_cat], axis=0), 0.0).astype(BF16)

    def mm(a_cat, b_cat):
        return jnp.dot(a_cat.astype(BF16), block_diag(b_cat), preferred_element_type=F32)

    def lanes(g):
        return pl.ds(g * P, P) if isinstance(g, int) else pl.ds(pl.multiple_of(g * P, P), P)

    def chunk(sl, s_blk):
        r = r_ref[:, sl]
        k = k_ref[:, sl]
        v = v_ref[:, sl]
        wraw = wl_ref[:, sl] + w0_ref[:, sl]
        w_log = -(jnp.maximum(-wraw, 0.0) + _softplus_neg_abs(wraw)) - 0.5
        ld = -jnp.exp(w_log)
        cl2 = jnp.dot(tri, jnp.concatenate(_split_bf16(ld, 2), axis=1), preferred_element_type=F32)
        yield
        a = _sigmoid(al_ref[:, sl] + a0_ref[:, sl])
        kkr = k * kkw_ref[:, sl]
        kk_ss = seg_sum(kkr * kkr)
        yield
        kk = kkr * lax.rsqrt(jnp.maximum(kk_ss, 1e-24))
        kh = k * (1.0 + (a - 1.0) * ka_ref[:, sl])
        b = -(kk * a)
        bonus = seg_sum(r * kh * rk_ref[:, sl])
        yield
        cl = cl2[:, :P] + cl2[:, P:]
        cl_last = cl[L - 1:L, :]
        e_inv = jnp.exp(-cl)
        gam = jnp.exp(cl_last)
        e_tail = gam * e_inv
        lhs = jnp.concatenate([kk * jnp.exp(cl - ld), r * jnp.exp(cl)], axis=0)
        rhs = jnp.concatenate([b * e_inv, kh * e_inv], axis=0).astype(BF16)
        res0 = lax.dot_general(jnp.where(lane_head0, lhs, 0.0).astype(BF16), rhs, nt, preferred_element_type=F32)
        yield
        res1 = lax.dot_general(jnp.where(lane_head0, 0.0, lhs).astype(BF16), rhs, nt, preferred_element_type=F32)
        yield
        ls = lax.dot_general(lhs.astype(BF16), s_blk.astype(BF16), nt, preferred_element_type=F32)
        yield
        pair = lambda blk_r, blk_c: jnp.concatenate(
            [res0[blk_r * L:(blk_r + 1) * L, blk_c * L:(blk_c + 1) * L],
             res1[blk_r * L:(blk_r + 1) * L, blk_c * L:(blk_c + 1) * L]], axis=1)
        nb = jnp.where(strict, pair(0, 0), 0.0)
        nk = jnp.where(strict, pair(0, 1), 0.0)
        mb = jnp.where(incl, pair(1, 0), 0.0)
        mk = jnp.where(incl, pair(1, 1), 0.0)
        v_blk = block_diag(v)
        rhs_rows = ls[:L] + jnp.dot(nk.astype(BF16), v_blk, preferred_element_type=F32)
        yield

        nd = jnp.where(diag_base, nb, 0.0)
        n2 = mm(nd, nd)
        yield
        n4 = mm(n2, n2)
        yield
        t_inv = eye + nd + n2 + mm(nd, n2)
        yield
        t_inv = t_inv + mm(t_inv, n4)
        yield
        for joined in merges:
            half = mm(t_inv, jnp.where(joined, nb, 0.0))
            yield
            t_inv = t_inv + mm(half, t_inv)
            yield
        sk = mm(t_inv, rhs_rows)
        yield

        y = ls[L:] + jnp.dot(jnp.concatenate([mb, mk], axis=1).astype(BF16),
                             jnp.concatenate([block_diag(sk), v_blk], axis=0), preferred_element_type=F32)
        yield
        d_s = lax.dot_general(jnp.concatenate([sk, v], axis=0).astype(BF16),
                              jnp.concatenate([b * e_tail, kh * e_tail], axis=0).astype(BF16),
                              (((0,), (0,)), ((), ())), preferred_element_type=F32)
        yield
        s_new = gam * s_blk + jnp.where(same_head, d_s, 0.0)

        mean = seg_sum(y) * (1.0 / N)
        yield
        yc = y - mean
        var = seg_sum(yc * yc) * (1.0 / N)
        yield
        yn = yc * lax.rsqrt(var + C_GN_EPS) * gng_ref[:, sl] + gnb_ref[:, sl]
        return s_new, ((yn + bonus * v) * g_ref[:, sl]).astype(o_ref.dtype)

    def sub_group(first):
        pairs = [first + g for g in range(SG)]
        results = _round_robin([chunk(lanes(p), s_ref[p]) for p in pairs])
        for p, (s_new, out) in zip(pairs, results):
            s_ref[p] = s_new
            o_ref[:, lanes(p)] = out

    if G == SG:
        sub_group(0)
    else:
        def body(sg, carry):
            sub_group(sg * SG)
            return carry

        lax.fori_loop(0, G // SG, body, 0)


def rwkv7_scan(r, k, v, wl, al, g, w0, a0, k_k, k_a, r_k, gn_g, gn_b, *, bsz, seq, G=32, SG=8):
    m, d = r.shape
    L = C_HEAD_DIM
    P = 2 * C_HEAD_DIM
    nc = seq // L
    wide = G * P
    tile = pl.BlockSpec((L, wide), lambda b, j, c: (b * nc + c, j))
    vec = pl.BlockSpec((1, wide), lambda b, j, c: (0, j))
    vecs = [t.reshape(1, d).astype(F32) for t in (w0, a0, k_k, k_a, r_k, gn_g, gn_b)]
    return pl.pallas_call(
        functools.partial(_rwkv_kernel, L=L, G=G, SG=SG),
        grid=(bsz, d // wide, nc),
        in_specs=[tile] * 6 + [vec] * 7,
        out_specs=tile,
        out_shape=jax.ShapeDtypeStruct((m, d), BF16),
        scratch_shapes=[pltpu.VMEM((G, P, P), F32)],
        compiler_params=_params("parallel", "parallel", "arbitrary"),
        name="rwkv7_scan",
    )(r, k, v, wl, al, g, *vecs)


def rwkv7_mixer(x, mu, w_rkv, w0, w1, w2, a0, a1, a2, g1, g2, k_k, k_a, r_k, gn_g, gn_b, w_out, *, bsz, seq):
    xr, xw, xk, xv, xa, xg = token_shift_mix(x, mu, seq=seq)
    r = matmul(xr, w_rkv[0].astype(BF16), name="rwkv_r")
    k = matmul(xk, w_rkv[1].astype(BF16), name="rwkv_k")
    v = matmul(xv, w_rkv[2].astype(BF16), name="rwkv_v")
    wl = matmul(matmul(xw, w1.astype(BF16), act="tanh", out_dtype=BF16, name="rwkv_w1"), w2.astype(BF16),
                name="rwkv_w2")
    al = matmul(matmul(xa, a1.astype(BF16), out_dtype=BF16, name="rwkv_a1"), a2.astype(BF16), name="rwkv_a2")
    n_g = g1.shape[1]
    n_gp = -(-n_g // 128) * 128
    g1p = jnp.zeros((g1.shape[0], n_gp), BF16).at[:, :n_g].set(g1.astype(BF16))
    g2p = jnp.zeros((n_gp, g2.shape[1]), BF16).at[:n_g, :].set(g2.astype(BF16))
    g = matmul(matmul(xg, g1p, act="sigmoid", out_dtype=BF16, name="rwkv_g1"), g2p, name="rwkv_g2")
    y = rwkv7_scan(r, k, v, wl, al, g, w0, a0, k_k, k_a, r_k, gn_g, gn_b, bsz=bsz, seq=seq)
    return matmul(y, w_out.astype(BF16), name="rwkv_out")


def kernel(x, norm_g, mlp_w1, mlp_w2, a_w_in, a_b_if, a_norm_g, a_w_out, b_w_in, b_w_out, c_mu, c_w_rkv,
           c_w0, c_w1, c_w2, c_a0, c_a1, c_a2, c_g1, c_g2, c_k_k, c_k_a, c_r_k, c_gn_g, c_gn_b, c_w_out):
    bsz, seq, d = x.shape
    depth = norm_g.shape[0]
    h = x.reshape(bsz * seq, d)
    kinds = [layer % 3 for layer in range(depth)]
    pre_dtype = lambda kind: F32 if kind == 2 else BF16
    y = rms_norm(h, norm_g[0, 0], out_dtype=pre_dtype(kinds[0]))
    for layer in range(depth):
        kind, idx = kinds[layer], layer // 3
        if kind == 0:
            u = mlstm_mixer(y, a_w_in[idx], a_b_if[idx], a_norm_g[idx], a_w_out[idx], bsz=bsz, seq=seq)
        elif kind == 1:
            u = stick_breaking_mixer(y, b_w_in[idx], b_w_out[idx], bsz=bsz, seq=seq)
        else:
            u = rwkv7_mixer(y, c_mu[idx], c_w_rkv[idx], c_w0[idx], c_w1[idx], c_w2[idx], c_a0[idx], c_a1[idx],
                            c_a2[idx], c_g1[idx], c_g2[idx], c_k_k[idx], c_k_a[idx], c_r_k[idx], c_gn_g[idx],
                            c_gn_b[idx], c_w_out[idx], bsz=bsz, seq=seq)
        h, y = add_rms(h, u, norm_g[layer, 1], norm_g[layer, 2])
        mid = matmul(y, mlp_w1[layer].astype(BF16), act="relu2", out_dtype=BF16, name="mlp_up")
        u = matmul(mid, mlp_w2[layer].astype(BF16), name="mlp_down")
        last = layer == depth - 1
        h, y = add_rms(h, u, norm_g[layer, 3], None if last else norm_g[layer + 1, 0],
                       pre_dtype=BF16 if last else pre_dtype(kinds[layer + 1]))
    return h.reshape(bsz, seq, d)
```

```python
import functools

import jax
import jax.numpy as jnp
from jax import lax
from jax.experimental import pallas as pl
from jax.experimental.pallas import tpu as pltpu

F32 = jnp.float32
BF16 = jnp.bfloat16

EPS = 1e-6
VMEM_LIMIT_BYTES = 56 * 1024 * 1024
MM_TILE_BYTES = 8 * 1024 * 1024

A_HEADS = 8
A_GATE_CAP = 15.0
B_HEADS = 32
Q_BLOCK = 128
C_HEAD_DIM = 64
C_GN_EPS = 64e-5
SB_UNDERFLOW = -104.0


def _params(*sem):
    return pltpu.CompilerParams(dimension_semantics=sem, vmem_limit_bytes=VMEM_LIMIT_BYTES)


def _softplus_neg_abs(z):
    return jnp.log(1.0 + jnp.exp(-jnp.abs(z)))


def _log_sigmoid(z):
    return jnp.minimum(z, 0.0) - _softplus_neg_abs(z)


def _sigmoid(z):
    return 1.0 / (1.0 + jnp.exp(-z))


def _round_robin(gens):
    results = [None] * len(gens)
    active = list(enumerate(gens))
    while active:
        still = []
        for i, gen in active:
            try:
                next(gen)
                still.append((i, gen))
            except StopIteration as stop:
                results[i] = stop.value
        active = still
    return results


_ACTS = {
    "none": lambda r: r,
    "relu2": lambda r: jnp.square(jnp.maximum(r, 0.0)),
    "sigmoid": _sigmoid,
    "tanh": jnp.tanh,
}


def _mm_kernel(a_ref, w_ref, o_ref, *scratch, act, nk):
    part = jnp.dot(a_ref[...].astype(BF16), w_ref[...].astype(BF16), preferred_element_type=F32)
    if nk == 1:
        o_ref[...] = _ACTS[act](part).astype(o_ref.dtype)
        return
    acc_ref, = scratch
    k = pl.program_id(2)

    @pl.when(k == 0)
    def _():
        acc_ref[...] = part

    @pl.when(jnp.logical_and(k > 0, k < nk - 1))
    def _():
        acc_ref[...] += part

    @pl.when(k == nk - 1)
    def _():
        o_ref[...] = _ACTS[act](acc_ref[...] + part).astype(o_ref.dtype)


def matmul(a, w, *, lead=(), n=None, act="none", out_dtype=F32, tm=1024, tn=None, tk=4096, name="matmul"):
    m, k = a.shape
    k2, n_w = w.shape[len(lead):]
    n = n_w if n is None else n
    assert k == k2 and n <= n_w and w.ndim == len(lead) + 2
    if tn is None:
        tn = min(1024, MM_TILE_BYTES // (min(tk, k) * w.dtype.itemsize))
    tm, tn = min(tm, m), min(tn, n)
    tk = min(tk, k, MM_TILE_BYTES // (tm * a.dtype.itemsize))
    assert m % tm == 0 and n % tn == 0 and k % tk == 0, (a.shape, w.shape)
    nk = k // tk
    return pl.pallas_call(
        functools.partial(_mm_kernel, act=act, nk=nk),
        grid=(m // tm, n // tn, nk),
        in_specs=[pl.BlockSpec((tm, tk), lambda i, j, kk: (i, kk)),
                  pl.BlockSpec((None,) * len(lead) + (tk, tn), lambda i, j, kk: (*lead, kk, j))],
        out_specs=pl.BlockSpec((tm, tn), lambda i, j, kk: (i, j)),
        out_shape=jax.ShapeDtypeStruct((m, n), out_dtype),
        scratch_shapes=[pltpu.VMEM((tm, tn), F32)] if nk > 1 else [],
        compiler_params=_params("parallel", "parallel", "arbitrary"),
        name=name,
    )(a, w)


def _rms(x, g):
    return x * lax.rsqrt(jnp.mean(x * x, axis=-1, keepdims=True) + EPS) * g


def _rms_kernel(x_ref, g_ref, o_ref):
    o_ref[...] = _rms(x_ref[...], g_ref[...]).astype(o_ref.dtype)


def rms_norm(x, g, *, out_dtype=BF16, tr=256):
    m, d = x.shape
    tr = min(tr, m)
    return pl.pallas_call(
        _rms_kernel,
        grid=(m // tr,),
        in_specs=[pl.BlockSpec((tr, d), lambda i: (i, 0)), pl.BlockSpec((1, d), lambda i: (0, 0))],
        out_specs=pl.BlockSpec((tr, d), lambda i: (i, 0)),
        out_shape=jax.ShapeDtypeStruct((m, d), out_dtype),
        compiler_params=_params("parallel"),
        name="rms_norm",
    )(x, g.reshape(1, d))


def _add_rms_kernel(h_ref, u_ref, gpost_ref, *rest, with_pre):
    hn = h_ref[...] + _rms(u_ref[...].astype(F32), gpost_ref[...])
    if with_pre:
        gpre_ref, hn_ref, y_ref = rest
        y_ref[...] = _rms(hn, gpre_ref[...]).astype(y_ref.dtype)
    else:
        hn_ref, = rest
    hn_ref[...] = hn


def add_rms(h, u, g_post, g_pre=None, *, pre_dtype=BF16, tr=128):
    m, d = h.shape
    tr = min(tr, m)
    with_pre = g_pre is not None
    row = pl.BlockSpec((tr, d), lambda i: (i, 0))
    vec = pl.BlockSpec((1, d), lambda i: (0, 0))
    args = [h, u, g_post.reshape(1, d)] + ([g_pre.reshape(1, d)] if with_pre else [])
    out_shape = [jax.ShapeDtypeStruct((m, d), F32)] + ([jax.ShapeDtypeStruct((m, d), pre_dtype)] if with_pre else [])
    outs = pl.pallas_call(
        functools.partial(_add_rms_kernel, with_pre=with_pre),
        grid=(m // tr,),
        in_specs=[row, row, vec] + ([vec] if with_pre else []),
        out_specs=[row] * len(out_shape),
        out_shape=out_shape,
        compiler_params=_params("parallel"),
        name="add_rms",
    )(*args)
    return outs if with_pre else (outs[0], None)


def _mlstm_gate_kernel(x_ref, w_ref, b_ref, o_ref):
    g = jnp.dot(x_ref[...], w_ref[...], preferred_element_type=F32) + b_ref[...]
    lane = lax.broadcasted_iota(jnp.int32, g.shape, 1)
    ig = A_GATE_CAP * jnp.tanh(g / A_GATE_CAP)
    o_ref[...] = jnp.where(lane < A_HEADS, ig, _log_sigmoid(g))


def mlstm_gates(x, w_gate, b_if, *, tr=512):
    m, d = x.shape
    tr = min(tr, m)
    ng = 128
    w = jnp.zeros((d, ng), BF16).at[:, :2 * A_HEADS].set(w_gate.astype(BF16))
    b = jnp.zeros((1, ng), F32).at[0, :2 * A_HEADS].set(b_if.astype(F32))
    return pl.pallas_call(
        _mlstm_gate_kernel,
        grid=(m // tr,),
        in_specs=[pl.BlockSpec((tr, d), lambda i: (i, 0)), pl.BlockSpec((d, ng), lambda i: (0, 0)),
                  pl.BlockSpec((1, ng), lambda i: (0, 0))],
        out_specs=pl.BlockSpec((tr, ng), lambda i: (i, 0)),
        out_shape=jax.ShapeDtypeStruct((m, ng), F32),
        compiler_params=_params("parallel"),
        name="mlstm_gates",
    )(x, w, b)


def _mlstm_kernel(q_ref, k_ref, v_ref, og_ref, igr_ref, lfr_ref, g_ref,
                  out_ref, c_ref, n_ref, m_ref, *, L, hg, qk_scale):
    @pl.when(pl.program_id(2) == 0)
    def _():
        c_ref[...] = jnp.zeros_like(c_ref)
        n_ref[...] = jnp.zeros_like(n_ref)
        m_ref[...] = jnp.zeros_like(m_ref)

    dqk = q_ref.shape[1] // hg
    dv = v_ref.shape[1] // hg
    row = lax.broadcasted_iota(jnp.int32, (L, L), 0)
    col = lax.broadcasted_iota(jnp.int32, (L, L), 1)
    causal = col <= row
    tri = causal.astype(F32)
    tri_t = (row <= col).astype(F32)
    eye = row == col

    def head(h):
        qk_cols = slice(h * dqk, (h + 1) * dqk)
        v_cols = slice(h * dv, (h + 1) * dv)
        q = q_ref[:, qk_cols] * qk_scale
        k = k_ref[:, qk_cols]
        v = v_ref[:, v_cols]
        ig_row, lf_row = igr_ref[h], lfr_ref[h]
        ig_col = jnp.sum(jnp.where(eye, ig_row, 0.0), axis=1, keepdims=True)
        lf_col = jnp.sum(jnp.where(eye, lf_row, 0.0), axis=1, keepdims=True)
        m_prev = m_ref[h]

        b_col = jnp.sum(tri * lf_row, axis=1, keepdims=True)
        b_row = jnp.sum(tri_t * lf_col, axis=0, keepdims=True)
        b_last = jnp.sum(lf_row, axis=1, keepdims=True)

        d = jnp.where(causal, b_col - b_row + ig_row, -jnp.inf)
        inter = b_col + m_prev
        m_t = jnp.maximum(jnp.max(d, axis=1, keepdims=True), inter)
        qk = lax.dot_general(q, k, (((1,), (1,)), ((), ())), preferred_element_type=F32)
        yield
        s = qk * jnp.exp(d - m_t)
        inter_w = jnp.exp(inter - m_t)
        c_prev = c_ref[h]
        n_prev = n_ref[h]
        sv = jnp.dot(s.astype(BF16), v, preferred_element_type=F32)
        yield
        qc = jnp.dot(q, c_prev.astype(BF16), preferred_element_type=F32)
        yield
        num = sv + inter_w * qc
        qn = jnp.sum(q.astype(F32) * n_prev, axis=1, keepdims=True)
        den = jnp.sum(s, axis=1, keepdims=True) + inter_w * qn
        hid = num / jnp.maximum(jnp.abs(den), jnp.exp(-m_t))

        a_col = b_last - b_col + ig_col
        a_row = b_last - b_row + ig_row
        m_new = jnp.maximum(b_last + m_prev, jnp.max(a_row, axis=1, keepdims=True))
        w_col = jnp.exp(a_col - m_new)
        decay = jnp.exp(b_last + m_prev - m_new)
        kw = k.astype(F32) * w_col
        kv = lax.dot_general(kw.astype(BF16), v, (((0,), (0,)), ((), ())), preferred_element_type=F32)
        yield
        c_new = decay * c_prev + kv
        n_new = decay * n_prev + jnp.sum(kw, axis=0, keepdims=True)

        hn = hid * lax.rsqrt(jnp.mean(hid * hid, axis=1, keepdims=True) + EPS) * g_ref[:, v_cols]
        out = (hn * _sigmoid(og_ref[:, v_cols].astype(F32))).astype(out_ref.dtype)
        return c_new, n_new, m_new, out

    for h, (c_new, n_new, m_new, out) in enumerate(_round_robin([head(h) for h in range(hg)])):
        c_ref[h] = c_new
        n_ref[h] = n_new
        m_ref[h] = m_new
        out_ref[:, h * dv:(h + 1) * dv] = out


def mlstm_scan(proj, gates, norm_g, *, bsz, seq, L=128, hg=4):
    h_ = A_HEADS
    d_model = norm_g.shape[0]
    dv = d_model // h_
    dqk = dv // 2
    nc = seq // L
    ng = h_ // hg
    g4 = gates[:, :2 * h_].reshape(bsz, nc, L, 2 * h_).transpose(0, 3, 1, 2)
    ig, lf = g4[:, :h_], g4[:, h_:]
    rowv = lambda t: t[..., None, :]
    row_spec = pl.BlockSpec((None, hg, None, 1, L), lambda b, h, c: (b, h, c, 0, 0))
    kq = 2 * h_ * dqk // dv
    return pl.pallas_call(
        functools.partial(_mlstm_kernel, L=L, hg=hg, qk_scale=dqk ** -0.5),
        grid=(bsz, ng, nc),
        in_specs=[pl.BlockSpec((L, hg * dqk), lambda b, h, c: (b * nc + c, h)),
                  pl.BlockSpec((L, hg * dqk), lambda b, h, c: (b * nc + c, ng + h)),
                  pl.BlockSpec((L, hg * dv), lambda b, h, c: (b * nc + c, kq // hg + h)),
                  pl.BlockSpec((L, hg * dv), lambda b, h, c: (b * nc + c, (kq + h_) // hg + h)),
                  row_spec, row_spec,
                  pl.BlockSpec((1, hg * dv), lambda b, h, c: (0, h))],
        out_specs=pl.BlockSpec((L, hg * dv), lambda b, h, c: (b * nc + c, h)),
        out_shape=jax.ShapeDtypeStruct((bsz * seq, d_model), BF16),
        scratch_shapes=[pltpu.VMEM((hg, dqk, dv), F32), pltpu.VMEM((hg, 1, dqk), F32),
                        pltpu.VMEM((hg, 1, 1), F32)],
        compiler_params=_params("parallel", "parallel", "arbitrary"),
        name="mlstm_scan",
    )(proj, proj, proj, proj, rowv(ig), rowv(lf), norm_g.reshape(1, d_model).astype(F32))


def mlstm_mixer(x, w_in, b_if, norm_g, w_out, idx, *, bsz, seq):
    n_main = w_in.shape[2] - 2 * A_HEADS
    proj = matmul(x, w_in, lead=(idx,), n=n_main, out_dtype=BF16, name="mlstm_in")
    gates = mlstm_gates(x, w_in[idx, :, n_main:], b_if[idx])
    hg = mlstm_scan(proj, gates, norm_g[idx], bsz=bsz, seq=seq)
    return matmul(hg, w_out, lead=(idx,), out_dtype=BF16, name="mlstm_out")


def _sb_kernel(q_ref, k_ref, v_ref, o_ref, *, scale, hg):
    T = Q_BLOCK
    dh = q_ref.shape[1] // hg
    qi = pl.program_id(2)
    row = lax.broadcasted_iota(jnp.int32, (T, T), 0)
    col = lax.broadcasted_iota(jnp.int32, (T, T), 1)
    strict = col < row
    upper = (row > col).astype(BF16)
    upper2 = jnp.concatenate([upper, upper], axis=0)

    def head_block(q, k, v, carry, acc, masked):
        z = lax.dot_general(q, k, (((1,), (1,)), ((), ())), preferred_element_type=F32) * scale
        yield
        sp = _softplus_neg_abs(z)
        ls_pos = jnp.minimum(z, 0.0) - sp
        lk = jnp.minimum(-z, 0.0) - sp
        if masked:
            lk = jnp.where(strict, lk, 0.0)
        hi = lk.astype(BF16)
        lo = (lk - hi.astype(F32)).astype(BF16)
        after = jnp.dot(jnp.concatenate([hi, lo], axis=1), upper2, preferred_element_type=F32)
        yield
        w = jnp.exp(ls_pos + after + carry)
        if masked:
            w = jnp.where(strict, w, 0.0)
        acc = acc + jnp.dot(w.astype(BF16), v, preferred_element_type=F32)
        carry = carry + jnp.sum(lk, axis=1, keepdims=True)
        return carry, acc

    heads = [slice(h * dh, (h + 1) * dh) for h in range(hg)]
    qs = [q_ref[:, sl] for sl in heads]

    def block(kb, carries, accs, masked):
        start = pl.multiple_of(kb * T, T)
        out = _round_robin([head_block(qs[h], k_ref[pl.ds(start, T), heads[h]], v_ref[pl.ds(start, T), heads[h]],
                                       carries[h], accs[h], masked) for h in range(hg)])
        return tuple(o[0] for o in out), tuple(o[1] for o in out)

    carries, accs = block(qi, (jnp.zeros((T, 1), F32),) * hg, (jnp.zeros((T, dh), F32),) * hg, True)

    def live(carries):
        worst = functools.reduce(jnp.maximum, carries)
        return jnp.max(worst) >= SB_UNDERFLOW

    def cond(state):
        j, alive, _, _ = state
        return jnp.logical_and(j < qi, alive)

    def body(state):
        j, _, carries, accs = state
        carries, accs = block(qi - 1 - j, carries, accs, False)
        return j + 1, live(carries), carries, accs

    _, _, _, accs = lax.while_loop(cond, body, (jnp.int32(0), live(carries), carries, accs))
    for h in range(hg):
        o_ref[:, heads[h]] = accs[h].astype(o_ref.dtype)


def stick_breaking_scan(qkv, *, bsz, seq, hg=8):
    d_model = qkv.shape[1] // 3
    dh = d_model // B_HEADS
    nq = seq // Q_BLOCK
    ng = B_HEADS // hg
    wide = hg * dh
    return pl.pallas_call(
        functools.partial(_sb_kernel, scale=dh ** -0.5, hg=hg),
        grid=(bsz, ng, nq),
        in_specs=[pl.BlockSpec((Q_BLOCK, wide), lambda b, h, i: (b * nq + i, h)),
                  pl.BlockSpec((seq, wide), lambda b, h, i: (b, ng + h)),
                  pl.BlockSpec((seq, wide), lambda b, h, i: (b, 2 * ng + h))],
        out_specs=pl.BlockSpec((Q_BLOCK, wide), lambda b, h, i: (b * nq + i, h)),
        out_shape=jax.ShapeDtypeStruct((bsz * seq, d_model), BF16),
        compiler_params=_params("parallel", "parallel", "arbitrary"),
        name="stick_breaking_scan",
    )(qkv, qkv, qkv)


def stick_breaking_mixer(x, w_in, w_out, idx, *, bsz, seq):
    qkv = matmul(x, w_in, lead=(idx,), out_dtype=BF16, name="sb_in")
    o = stick_breaking_scan(qkv, bsz=bsz, seq=seq)
    return matmul(o, w_out, lead=(idx,), out_dtype=BF16, name="sb_out")


def _shift_kernel(x_ref, prev_ref, mu_ref, *o_refs, tiles_per_seq):
    x = x_ref[...]
    first = pl.program_id(0) % tiles_per_seq == 0
    prev_last = jnp.where(first, 0.0, prev_ref[7:8, :])
    row = lax.broadcasted_iota(jnp.int32, x.shape, 0)
    x_prev = jnp.where(row == 0, prev_last, pltpu.roll(x, 1, 0))
    xx = x_prev - x
    for i, o_ref in enumerate(o_refs):
        o_ref[...] = (x + xx * mu_ref[i:i + 1, :]).astype(o_ref.dtype)


def token_shift_mix(x, mu, *, seq, tr=256):
    m, d = x.shape
    tr = min(tr, seq)
    n_out = mu.shape[0]
    row = pl.BlockSpec((tr, d), lambda i: (i, 0))
    return pl.pallas_call(
        functools.partial(_shift_kernel, tiles_per_seq=seq // tr),
        grid=(m // tr,),
        in_specs=[row, pl.BlockSpec((8, d), lambda i: (jnp.maximum(i * (tr // 8) - 1, 0), 0)),
                  pl.BlockSpec((n_out, d), lambda i: (0, 0))],
        out_specs=[row] * n_out,
        out_shape=[jax.ShapeDtypeStruct((m, d), BF16)] * n_out,
        compiler_params=_params("parallel"),
        name="token_shift_mix",
    )(x, x, mu)


def _split_bf16(x, terms):
    parts = []
    for _ in range(terms - 1):
        p = x.astype(BF16)
        parts.append(p)
        x = x - p.astype(F32)
    parts.append(x.astype(BF16))
    return parts


def _rwkv_kernel(r_ref, k_ref, v_ref, wl_ref, al_ref, g_ref, w0_ref, a0_ref, kkw_ref, ka_ref, rk_ref,
                 gng_ref, gnb_ref, o_ref, s_ref, *, L, G, SG):
    N = C_HEAD_DIM
    P = 2 * N
    assert L == N and P == 128

    @pl.when(pl.program_id(2) == 0)
    def _():
        s_ref[...] = jnp.zeros_like(s_ref)

    r128 = lax.broadcasted_iota(jnp.int32, (P, P), 0)
    c128 = lax.broadcasted_iota(jnp.int32, (P, P), 1)
    same_head = (r128 // N) == (c128 // N)
    ones_blk = same_head.astype(BF16)
    t_i = lax.broadcasted_iota(jnp.int32, (L, P), 0)
    s_i = lax.broadcasted_iota(jnp.int32, (L, P), 1) % N
    strict = s_i < t_i
    incl = s_i <= t_i
    eye = (s_i == t_i).astype(F32)
    same_block = lambda width: (t_i // width) == (s_i // width)
    base = 8
    diag_base = same_block(base)
    merges = []
    width = base
    while width < L:
        merges.append(jnp.logical_and(same_block(2 * width), jnp.logical_not(same_block(width))))
        width *= 2
    tri = (lax.broadcasted_iota(jnp.int32, (L, L), 1) <= lax.broadcasted_iota(jnp.int32, (L, L), 0)).astype(BF16)
    lane_head0 = lax.broadcasted_iota(jnp.int32, (1, P), 1) < N
    nt = (((1,), (1,)), ((), ()))

    def seg_sum(x):
        return jnp.dot(x.astype(BF16), ones_blk, preferred_element_type=F32)

    def block_diag(x_cat):
        return jnp.where(same_head, jnp.concatenate([x_cat, x_cat], axis=0), 0.0).astype(BF16)

    def mm(a_cat, b_cat):
        return jnp.dot(a_cat.astype(BF16), block_diag(b_cat), preferred_element_type=F32)

    def lanes(g):
        return pl.ds(g * P, P) if isinstance(g, int) else pl.ds(pl.multiple_of(g * P, P), P)

    def chunk(sl, s_blk):
        r = r_ref[:, sl].astype(F32)
        k = k_ref[:, sl]
        v = v_ref[:, sl].astype(F32)
        wraw = wl_ref[:, sl] + w0_ref[:, sl]
        w_log = -(jnp.maximum(-wraw, 0.0) + _softplus_neg_abs(wraw)) - 0.5
        ld = -jnp.exp(w_log)
        cl2 = jnp.dot(tri, jnp.concatenate(_split_bf16(ld, 2), axis=1), preferred_element_type=F32)
        yield
        a = _sigmoid(al_ref[:, sl].astype(F32) + a0_ref[:, sl])
        kkr = k * kkw_ref[:, sl]
        kk_ss = seg_sum(kkr * kkr)
        yield
        kk = kkr * lax.rsqrt(jnp.maximum(kk_ss, 1e-24))
        kh = k * (1.0 + (a - 1.0) * ka_ref[:, sl])
        b = -(kk * a)
        bonus = seg_sum(r * kh * rk_ref[:, sl])
        yield
        cl = cl2[:, :P] + cl2[:, P:]
        cl_last = cl[L - 1:L, :]
        e_inv = jnp.exp(-cl)
        gam = jnp.exp(cl_last)
        e_tail = gam * e_inv
        lhs = jnp.concatenate([kk * jnp.exp(cl - ld), r * jnp.exp(cl)], axis=0)
        rhs = jnp.concatenate([b * e_inv, kh * e_inv], axis=0).astype(BF16)
        res0 = lax.dot_general(jnp.where(lane_head0, lhs, 0.0).astype(BF16), rhs, nt, preferred_element_type=F32)
        yield
        res1 = lax.dot_general(jnp.where(lane_head0, 0.0, lhs).astype(BF16), rhs, nt, preferred_element_type=F32)
        yield
        ls = lax.dot_general(lhs.astype(BF16), s_blk.astype(BF16), nt, preferred_element_type=F32)
        yield
        pair = lambda blk_r, blk_c: jnp.concatenate(
            [res0[blk_r * L:(blk_r + 1) * L, blk_c * L:(blk_c + 1) * L],
             res1[blk_r * L:(blk_r + 1) * L, blk_c * L:(blk_c + 1) * L]], axis=1)
        nb = jnp.where(strict, pair(0, 0), 0.0)
        nk = jnp.where(strict, pair(0, 1), 0.0)
        mb = jnp.where(incl, pair(1, 0), 0.0)
        mk = jnp.where(incl, pair(1, 1), 0.0)
        v_blk = block_diag(v)
        rhs_rows = ls[:L] + jnp.dot(nk.astype(BF16), v_blk, preferred_element_type=F32)
        yield

        nd = jnp.where(diag_base, nb, 0.0)
        n2 = mm(nd, nd)
        yield
        n4 = mm(n2, n2)
        yield
        t_inv = eye + nd + n2 + mm(nd, n2)
        yield
        t_inv = t_inv + mm(t_inv, n4)
        yield
        for joined in merges:
            half = mm(t_inv, jnp.where(joined, nb, 0.0))
            yield
            t_inv = t_inv + mm(half, t_inv)
            yield
        sk = mm(t_inv, rhs_rows)
        yield

        y = ls[L:] + jnp.dot(jnp.concatenate([mb, mk], axis=1).astype(BF16),
                             jnp.concatenate([block_diag(sk), v_blk], axis=0), preferred_element_type=F32)
        yield
        d_s = lax.dot_general(jnp.concatenate([sk, v], axis=0).astype(BF16),
                              jnp.concatenate([b * e_tail, kh * e_tail], axis=0).astype(BF16),
                              (((0,), (0,)), ((), ())), preferred_element_type=F32)
        yield
        s_new = gam * s_blk + jnp.where(same_head, d_s, 0.0)

        mean = seg_sum(y) * (1.0 / N)
        yield
        yc = y - mean
        var = seg_sum(yc * yc) * (1.0 / N)
        yield
        yn = yc * lax.rsqrt(var + C_GN_EPS) * gng_ref[:, sl] + gnb_ref[:, sl]
        return s_new, ((yn + bonus * v) * g_ref[:, sl].astype(F32)).astype(o_ref.dtype)

    def sub_group(first):
        pairs = [first + g for g in range(SG)]
        results = _round_robin([chunk(lanes(p), s_ref[p]) for p in pairs])
        for p, (s_new, out) in zip(pairs, results):
            s_ref[p] = s_new
            o_ref[:, lanes(p)] = out

    if G == SG:
        sub_group(0)
    else:
        def body(sg, carry):
            sub_group(sg * SG)
            return carry

        lax.fori_loop(0, G // SG, body, 0)


def rwkv7_scan(r, k, v, wl, al, g, w0, a0, k_k, k_a, r_k, gn_g, gn_b, *, bsz, seq, G=32, SG=16):
    m, d = r.shape
    L = C_HEAD_DIM
    P = 2 * C_HEAD_DIM
    nc = seq // L
    wide = G * P
    tile = pl.BlockSpec((L, wide), lambda b, j, c: (b * nc + c, j))
    vec = pl.BlockSpec((1, wide), lambda b, j, c: (0, j))
    vecs = [t.reshape(1, d).astype(F32) for t in (w0, a0, k_k, k_a, r_k, gn_g, gn_b)]
    return pl.pallas_call(
        functools.partial(_rwkv_kernel, L=L, G=G, SG=SG),
        grid=(bsz, d // wide, nc),
        in_specs=[tile] * 6 + [vec] * 7,
        out_specs=tile,
        out_shape=jax.ShapeDtypeStruct((m, d), BF16),
        scratch_shapes=[pltpu.VMEM((G, P, P), F32)],
        compiler_params=_params("parallel", "parallel", "arbitrary"),
        name="rwkv7_scan",
    )(r, k, v, wl, al, g, *vecs)


def rwkv7_mixer(x, mu, w_rkv, w0, w1, w2, a0, a1, a2, g1, g2, k_k, k_a, r_k, gn_g, gn_b, w_out, idx, *, bsz, seq):
    xr, xw, xk, xv, xa, xg = token_shift_mix(x, mu, seq=seq)
    r = matmul(xr, w_rkv, lead=(idx, 0), out_dtype=BF16, name="rwkv_r")
    k = matmul(xk, w_rkv, lead=(idx, 1), name="rwkv_k")
    v = matmul(xv, w_rkv, lead=(idx, 2), out_dtype=BF16, name="rwkv_v")
    wl = matmul(matmul(xw, w1.astype(BF16), act="tanh", out_dtype=BF16, name="rwkv_w1"), w2.astype(BF16),
                name="rwkv_w2")
    al = matmul(matmul(xa, a1.astype(BF16), out_dtype=BF16, name="rwkv_a1"), a2.astype(BF16), out_dtype=BF16,
                name="rwkv_a2")
    n_g = g1.shape[1]
    n_gp = -(-n_g // 128) * 128
    g1p = jnp.zeros((g1.shape[0], n_gp), BF16).at[:, :n_g].set(g1.astype(BF16))
    g2p = jnp.zeros((n_gp, g2.shape[1]), BF16).at[:n_g, :].set(g2.astype(BF16))
    g = matmul(matmul(xg, g1p, act="sigmoid", out_dtype=BF16, name="rwkv_g1"), g2p, out_dtype=BF16, name="rwkv_g2")
    y = rwkv7_scan(r, k, v, wl, al, g, w0, a0, k_k, k_a, r_k, gn_g, gn_b, bsz=bsz, seq=seq)
    return matmul(y, w_out, lead=(idx,), out_dtype=BF16, name="rwkv_out")


def kernel(x, norm_g, mlp_w1, mlp_w2, a_w_in, a_b_if, a_norm_g, a_w_out, b_w_in, b_w_out, c_mu, c_w_rkv,
           c_w0, c_w1, c_w2, c_a0, c_a1, c_a2, c_g1, c_g2, c_k_k, c_k_a, c_r_k, c_gn_g, c_gn_b, c_w_out):
    bsz, seq, d = x.shape
    depth = norm_g.shape[0]
    h = x.reshape(bsz * seq, d)
    kinds = [layer % 3 for layer in range(depth)]
    pre_dtype = lambda kind: F32 if kind == 2 else BF16
    y = rms_norm(h, norm_g[0, 0], out_dtype=pre_dtype(kinds[0]))
    for layer in range(depth):
        kind, idx = kinds[layer], layer // 3
        if kind == 0:
            u = mlstm_mixer(y, a_w_in, a_b_if, a_norm_g, a_w_out, idx, bsz=bsz, seq=seq)
        elif kind == 1:
            u = stick_breaking_mixer(y, b_w_in, b_w_out, idx, bsz=bsz, seq=seq)
        else:
            u = rwkv7_mixer(y, c_mu[idx], c_w_rkv, c_w0[idx], c_w1[idx], c_w2[idx], c_a0[idx], c_a1[idx],
                            c_a2[idx], c_g1[idx], c_g2[idx], c_k_k[idx], c_k_a[idx], c_r_k[idx], c_gn_g[idx],
                            c_gn_b[idx], c_w_out, idx, bsz=bsz, seq=seq)
        h, y = add_rms(h, u, norm_g[layer, 1], norm_g[layer, 2])
        mid = matmul(y, mlp_w1, lead=(layer,), act="relu2", out_dtype=BF16, name="mlp_up")
        u = matmul(mid, mlp_w2[layer].astype(BF16), out_dtype=BF16, name="mlp_down")
        last = layer == depth - 1
        h, y = add_rms(h, u, norm_g[layer, 3], None if last else norm_g[layer + 1, 0],
                       pre_dtype=BF16 if last else pre_dtype(kinds[layer + 1]))
    return h.reshape(bsz, seq, d)
```

```python
import functools

import jax
import jax.numpy as jnp
from jax import lax
from jax.experimental import pallas as pl
from jax.experimental.pallas import tpu as pltpu

F32 = jnp.float32
BF16 = jnp.bfloat16

EPS = 1e-6
VMEM_LIMIT_BYTES = 56 * 1024 * 1024
MM_TILE_BYTES = 8 * 1024 * 1024

A_HEADS = 8
A_GATE_CAP = 15.0
B_HEADS = 32
Q_BLOCK = 128
C_HEAD_DIM = 64
C_GN_EPS = 64e-5
SB_UNDERFLOW = -104.0


def _params(*sem):
    return pltpu.CompilerParams(dimension_semantics=sem, vmem_limit_bytes=VMEM_LIMIT_BYTES)


def _softplus_neg_abs(z):
    return jnp.log(1.0 + jnp.exp(-jnp.abs(z)))


def _log_sigmoid(z):
    return jnp.minimum(z, 0.0) - _softplus_neg_abs(z)


def _sigmoid(z):
    return 1.0 / (1.0 + jnp.exp(-z))


def _round_robin(gens):
    results = [None] * len(gens)
    active = list(enumerate(gens))
    while active:
        still = []
        for i, gen in active:
            try:
                next(gen)
                still.append((i, gen))
            except StopIteration as stop:
                results[i] = stop.value
        active = still
    return results


_ACTS = {
    "none": lambda r: r,
    "relu2": lambda r: jnp.square(jnp.maximum(r, 0.0)),
    "sigmoid": _sigmoid,
    "tanh": jnp.tanh,
}


def _mm_kernel(a_ref, w_ref, o_ref, *scratch, act, nk, w_t):
    w_contract = 1 if w_t else 0
    part = lax.dot_general(a_ref[...].astype(BF16), w_ref[...].astype(BF16), (((1,), (w_contract,)), ((), ())),
                           preferred_element_type=F32)
    if nk == 1:
        o_ref[...] = _ACTS[act](part).astype(o_ref.dtype)
        return
    acc_ref, = scratch
    k = pl.program_id(2)

    @pl.when(k == 0)
    def _():
        acc_ref[...] = part

    @pl.when(jnp.logical_and(k > 0, k < nk - 1))
    def _():
        acc_ref[...] += part

    @pl.when(k == nk - 1)
    def _():
        o_ref[...] = _ACTS[act](acc_ref[...] + part).astype(o_ref.dtype)


def matmul(a, w, *, lead=(), n=None, w_t=False, act="none", out_dtype=F32, tm=1024, tn=None, tk=4096,
           name="matmul"):
    m, k = a.shape
    k2, n_w = w.shape[len(lead):][::-1] if w_t else w.shape[len(lead):]
    n = n_w if n is None else n
    assert k == k2 and n <= n_w and w.ndim == len(lead) + 2
    if tn is None:
        tn = min(1024, MM_TILE_BYTES // (min(tk, k) * w.dtype.itemsize))
    tm, tn = min(tm, m), min(tn, n)
    tk = min(tk, k, MM_TILE_BYTES // (tm * a.dtype.itemsize))
    assert m % tm == 0 and n % tn == 0 and k % tk == 0, (a.shape, w.shape)
    nk = k // tk
    if w_t:
        w_spec = pl.BlockSpec((None,) * len(lead) + (tn, tk), lambda i, j, kk: (*lead, j, kk))
    else:
        w_spec = pl.BlockSpec((None,) * len(lead) + (tk, tn), lambda i, j, kk: (*lead, kk, j))
    return pl.pallas_call(
        functools.partial(_mm_kernel, act=act, nk=nk, w_t=w_t),
        grid=(m // tm, n // tn, nk),
        in_specs=[pl.BlockSpec((tm, tk), lambda i, j, kk: (i, kk)), w_spec],
        out_specs=pl.BlockSpec((tm, tn), lambda i, j, kk: (i, j)),
        out_shape=jax.ShapeDtypeStruct((m, n), out_dtype),
        scratch_shapes=[pltpu.VMEM((tm, tn), F32)] if nk > 1 else [],
        compiler_params=_params("parallel", "parallel", "arbitrary"),
        name=name,
    )(a, w)


def _rms(x, g):
    return x * lax.rsqrt(jnp.mean(x * x, axis=-1, keepdims=True) + EPS) * g


def _rms_kernel(x_ref, g_ref, o_ref):
    o_ref[...] = _rms(x_ref[...], g_ref[...]).astype(o_ref.dtype)


def rms_norm(x, g, *, out_dtype=BF16, tr=256):
    m, d = x.shape
    tr = min(tr, m)
    return pl.pallas_call(
        _rms_kernel,
        grid=(m // tr,),
        in_specs=[pl.BlockSpec((tr, d), lambda i: (i, 0)), pl.BlockSpec((1, d), lambda i: (0, 0))],
        out_specs=pl.BlockSpec((tr, d), lambda i: (i, 0)),
        out_shape=jax.ShapeDtypeStruct((m, d), out_dtype),
        compiler_params=_params("parallel"),
        name="rms_norm",
    )(x, g.reshape(1, d))


def _add_rms_kernel(h_ref, u_ref, gpost_ref, *rest, with_pre):
    hn = h_ref[...] + _rms(u_ref[...].astype(F32), gpost_ref[...])
    if with_pre:
        gpre_ref, hn_ref, y_ref = rest
        y_ref[...] = _rms(hn, gpre_ref[...]).astype(y_ref.dtype)
    else:
        hn_ref, = rest
    hn_ref[...] = hn


def add_rms(h, u, g_post, g_pre=None, *, pre_dtype=BF16, tr=128):
    m, d = h.shape
    tr = min(tr, m)
    with_pre = g_pre is not None
    row = pl.BlockSpec((tr, d), lambda i: (i, 0))
    vec = pl.BlockSpec((1, d), lambda i: (0, 0))
    args = [h, u, g_post.reshape(1, d)] + ([g_pre.reshape(1, d)] if with_pre else [])
    out_shape = [jax.ShapeDtypeStruct((m, d), F32)] + ([jax.ShapeDtypeStruct((m, d), pre_dtype)] if with_pre else [])
    outs = pl.pallas_call(
        functools.partial(_add_rms_kernel, with_pre=with_pre),
        grid=(m // tr,),
        in_specs=[row, row, vec] + ([vec] if with_pre else []),
        out_specs=[row] * len(out_shape),
        out_shape=out_shape,
        compiler_params=_params("parallel"),
        name="add_rms",
    )(*args)
    return outs if with_pre else (outs[0], None)


def _mlstm_gate_kernel(x_ref, w_ref, b_ref, o_ref):
    row_w = lax.broadcasted_iota(jnp.int32, w_ref.shape, 0)
    w = jnp.where(row_w < 2 * A_HEADS, w_ref[...], 0.0).astype(BF16)
    g = lax.dot_general(x_ref[...], w, (((1,), (1,)), ((), ())), preferred_element_type=F32) + b_ref[...]
    lane = lax.broadcasted_iota(jnp.int32, g.shape, 1)
    ig = A_GATE_CAP * jnp.tanh(g / A_GATE_CAP)
    o_ref[...] = jnp.where(lane < A_HEADS, ig, _log_sigmoid(g))


def mlstm_gates(x, w_in_t, b_if, idx, *, tr=512):
    m, d = x.shape
    tr = min(tr, m)
    ng = 128
    n_main = w_in_t.shape[1] - 2 * A_HEADS
    assert n_main % ng == 0
    b = jnp.zeros((1, ng), F32).at[0, :2 * A_HEADS].set(b_if[idx].astype(F32))
    return pl.pallas_call(
        _mlstm_gate_kernel,
        grid=(m // tr,),
        in_specs=[pl.BlockSpec((tr, d), lambda i: (i, 0)),
                  pl.BlockSpec((None, ng, d), lambda i: (idx, n_main // ng, 0)),
                  pl.BlockSpec((1, ng), lambda i: (0, 0))],
        out_specs=pl.BlockSpec((tr, ng), lambda i: (i, 0)),
        out_shape=jax.ShapeDtypeStruct((m, ng), F32),
        compiler_params=_params("parallel"),
        name="mlstm_gates",
    )(x, w_in_t, b)


def _mlstm_kernel(q_ref, k_ref, v_ref, og_ref, igr_ref, lfr_ref, g_ref,
                  out_ref, c_ref, n_ref, m_ref, *, L, hg, qk_scale):
    @pl.when(pl.program_id(2) == 0)
    def _():
        c_ref[...] = jnp.zeros_like(c_ref)
        n_ref[...] = jnp.zeros_like(n_ref)
        m_ref[...] = jnp.zeros_like(m_ref)

    dqk = q_ref.shape[1] // hg
    dv = v_ref.shape[1] // hg
    row = lax.broadcasted_iota(jnp.int32, (L, L), 0)
    col = lax.broadcasted_iota(jnp.int32, (L, L), 1)
    causal = col <= row
    tri = causal.astype(F32)
    tri_t = (row <= col).astype(F32)
    eye = row == col

    def head(h):
        qk_cols = slice(h * dqk, (h + 1) * dqk)
        v_cols = slice(h * dv, (h + 1) * dv)
        q = q_ref[:, qk_cols] * qk_scale
        k = k_ref[:, qk_cols]
        v = v_ref[:, v_cols]
        ig_row, lf_row = igr_ref[h], lfr_ref[h]
        ig_col = jnp.sum(jnp.where(eye, ig_row, 0.0), axis=1, keepdims=True)
        lf_col = jnp.sum(jnp.where(eye, lf_row, 0.0), axis=1, keepdims=True)
        m_prev = m_ref[h]

        b_col = jnp.sum(tri * lf_row, axis=1, keepdims=True)
        b_row = jnp.sum(tri_t * lf_col, axis=0, keepdims=True)
        b_last = jnp.sum(lf_row, axis=1, keepdims=True)

        d = jnp.where(causal, b_col - b_row + ig_row, -jnp.inf)
        inter = b_col + m_prev
        m_t = jnp.maximum(jnp.max(d, axis=1, keepdims=True), inter)
        qk = lax.dot_general(q, k, (((1,), (1,)), ((), ())), preferred_element_type=F32)
        yield
        s = qk * jnp.exp(d - m_t)
        inter_w = jnp.exp(inter - m_t)
        c_prev = c_ref[h]
        n_prev = n_ref[h]
        sv = jnp.dot(s.astype(BF16), v, preferred_element_type=F32)
        yield
        qc = jnp.dot(q, c_prev.astype(BF16), preferred_element_type=F32)
        yield
        num = sv + inter_w * qc
        qn = jnp.sum(q.astype(F32) * n_prev, axis=1, keepdims=True)
        den = jnp.sum(s, axis=1, keepdims=True) + inter_w * qn
        hid = num / jnp.maximum(jnp.abs(den), jnp.exp(-m_t))

        a_col = b_last - b_col + ig_col
        a_row = b_last - b_row + ig_row
        m_new = jnp.maximum(b_last + m_prev, jnp.max(a_row, axis=1, keepdims=True))
        w_col = jnp.exp(a_col - m_new)
        decay = jnp.exp(b_last + m_prev - m_new)
        kw = k.astype(F32) * w_col
        kv = lax.dot_general(kw.astype(BF16), v, (((0,), (0,)), ((), ())), preferred_element_type=F32)
        yield
        c_new = decay * c_prev + kv
        n_new = decay * n_prev + jnp.sum(kw, axis=0, keepdims=True)

        hn = hid * lax.rsqrt(jnp.mean(hid * hid, axis=1, keepdims=True) + EPS) * g_ref[:, v_cols]
        out = (hn * _sigmoid(og_ref[:, v_cols].astype(F32))).astype(out_ref.dtype)
        return c_new, n_new, m_new, out

    for h, (c_new, n_new, m_new, out) in enumerate(_round_robin([head(h) for h in range(hg)])):
        c_ref[h] = c_new
        n_ref[h] = n_new
        m_ref[h] = m_new
        out_ref[:, h * dv:(h + 1) * dv] = out


def mlstm_scan(proj, gates, norm_g, *, bsz, seq, L=128, hg=8):
    h_ = A_HEADS
    d_model = norm_g.shape[0]
    dv = d_model // h_
    dqk = dv // 2
    nc = seq // L
    ng = h_ // hg
    g4 = gates[:, :2 * h_].reshape(bsz, nc, L, 2 * h_).transpose(0, 3, 1, 2)
    ig, lf = g4[:, :h_], g4[:, h_:]
    rowv = lambda t: t[..., None, :]
    row_spec = pl.BlockSpec((None, hg, None, 1, L), lambda b, h, c: (b, h, c, 0, 0))
    kq = 2 * h_ * dqk // dv
    return pl.pallas_call(
        functools.partial(_mlstm_kernel, L=L, hg=hg, qk_scale=dqk ** -0.5),
        grid=(bsz, ng, nc),
        in_specs=[pl.BlockSpec((L, hg * dqk), lambda b, h, c: (b * nc + c, h)),
                  pl.BlockSpec((L, hg * dqk), lambda b, h, c: (b * nc + c, ng + h)),
                  pl.BlockSpec((L, hg * dv), lambda b, h, c: (b * nc + c, kq // hg + h)),
                  pl.BlockSpec((L, hg * dv), lambda b, h, c: (b * nc + c, (kq + h_) // hg + h)),
                  row_spec, row_spec,
                  pl.BlockSpec((1, hg * dv), lambda b, h, c: (0, h))],
        out_specs=pl.BlockSpec((L, hg * dv), lambda b, h, c: (b * nc + c, h)),
        out_shape=jax.ShapeDtypeStruct((bsz * seq, d_model), BF16),
        scratch_shapes=[pltpu.VMEM((hg, dqk, dv), F32), pltpu.VMEM((hg, 1, dqk), F32),
                        pltpu.VMEM((hg, 1, 1), F32)],
        compiler_params=_params("parallel", "parallel", "arbitrary"),
        name="mlstm_scan",
    )(proj, proj, proj, proj, rowv(ig), rowv(lf), norm_g.reshape(1, d_model).astype(F32))


def mlstm_mixer(x, w_in, b_if, norm_g, w_out, idx, *, bsz, seq):
    n_main = w_in.shape[2] - 2 * A_HEADS
    w_in_t = jnp.swapaxes(w_in, 1, 2)
    proj = matmul(x, w_in_t, lead=(idx,), n=n_main, w_t=True, out_dtype=BF16, name="mlstm_in")
    gates = mlstm_gates(x, w_in_t, b_if, idx)
    hg = mlstm_scan(proj, gates, norm_g[idx], bsz=bsz, seq=seq)
    return matmul(hg, w_out, lead=(idx,), out_dtype=BF16, name="mlstm_out")


def _sb_kernel(q_ref, k_ref, v_ref, o_ref, *, scale, hg):
    T = Q_BLOCK
    dh = q_ref.shape[1] // hg
    qi = pl.program_id(2)
    row = lax.broadcasted_iota(jnp.int32, (T, T), 0)
    col = lax.broadcasted_iota(jnp.int32, (T, T), 1)
    strict = col < row
    upper = (row > col).astype(BF16)
    upper2 = jnp.concatenate([upper, upper], axis=0)

    def head_block(q, k, v, carry, acc, masked):
        z = lax.dot_general(q, k, (((1,), (1,)), ((), ())), preferred_element_type=F32) * scale
        yield
        sp = _softplus_neg_abs(z)
        ls_pos = jnp.minimum(z, 0.0) - sp
        lk = jnp.minimum(-z, 0.0) - sp
        if masked:
            lk = jnp.where(strict, lk, 0.0)
        hi = lk.astype(BF16)
        lo = (lk - hi.astype(F32)).astype(BF16)
        after = jnp.dot(jnp.concatenate([hi, lo], axis=1), upper2, preferred_element_type=F32)
        yield
        w = jnp.exp(ls_pos + after + carry)
        if masked:
            w = jnp.where(strict, w, 0.0)
        acc = acc + jnp.dot(w.astype(BF16), v, preferred_element_type=F32)
        carry = carry + jnp.sum(lk, axis=1, keepdims=True)
        return carry, acc

    heads = [slice(h * dh, (h + 1) * dh) for h in range(hg)]
    qs = [q_ref[:, sl] for sl in heads]

    def block(kb, carries, accs, masked):
        start = pl.multiple_of(kb * T, T)
        out = _round_robin([head_block(qs[h], k_ref[pl.ds(start, T), heads[h]], v_ref[pl.ds(start, T), heads[h]],
                                       carries[h], accs[h], masked) for h in range(hg)])
        return tuple(o[0] for o in out), tuple(o[1] for o in out)

    carries, accs = block(qi, (jnp.zeros((T, 1), F32),) * hg, (jnp.zeros((T, dh), F32),) * hg, True)

    def live(carries):
        worst = functools.reduce(jnp.maximum, carries)
        return jnp.max(worst) >= SB_UNDERFLOW

    def cond(state):
        j, alive, _, _ = state
        return jnp.logical_and(j < qi, alive)

    def body(state):
        j, _, carries, accs = state
        carries, accs = block(qi - 1 - j, carries, accs, False)
        return j + 1, live(carries), carries, accs

    _, _, _, accs = lax.while_loop(cond, body, (jnp.int32(0), live(carries), carries, accs))
    for h in range(hg):
        o_ref[:, heads[h]] = accs[h].astype(o_ref.dtype)


def stick_breaking_scan(qkv, *, bsz, seq, hg=8):
    d_model = qkv.shape[1] // 3
    dh = d_model // B_HEADS
    nq = seq // Q_BLOCK
    ng = B_HEADS // hg
    wide = hg * dh
    return pl.pallas_call(
        functools.partial(_sb_kernel, scale=dh ** -0.5, hg=hg),
        grid=(bsz, ng, nq),
        in_specs=[pl.BlockSpec((Q_BLOCK, wide), lambda b, h, i: (b * nq + i, h)),
                  pl.BlockSpec((seq, wide), lambda b, h, i: (b, ng + h)),
                  pl.BlockSpec((seq, wide), lambda b, h, i: (b, 2 * ng + h))],
        out_specs=pl.BlockSpec((Q_BLOCK, wide), lambda b, h, i: (b * nq + i, h)),
        out_shape=jax.ShapeDtypeStruct((bsz * seq, d_model), BF16),
        compiler_params=_params("parallel", "parallel", "arbitrary"),
        name="stick_breaking_scan",
    )(qkv, qkv, qkv)


def stick_breaking_mixer(x, w_in, w_out, idx, *, bsz, seq):
    qkv = matmul(x, w_in, lead=(idx,), out_dtype=BF16, name="sb_in")
    o = stick_breaking_scan(qkv, bsz=bsz, seq=seq)
    return matmul(o, w_out, lead=(idx,), out_dtype=BF16, name="sb_out")


def _shift_kernel(x_ref, prev_ref, mu_ref, *o_refs, tiles_per_seq):
    x = x_ref[...]
    first = pl.program_id(0) % tiles_per_seq == 0
    prev_last = jnp.where(first, 0.0, prev_ref[7:8, :])
    row = lax.broadcasted_iota(jnp.int32, x.shape, 0)
    x_prev = jnp.where(row == 0, prev_last, pltpu.roll(x, 1, 0))
    xx = x_prev - x
    for i, o_ref in enumerate(o_refs):
        o_ref[...] = (x + xx * mu_ref[i:i + 1, :]).astype(o_ref.dtype)


def token_shift_mix(x, mu, *, seq, tr=256):
    m, d = x.shape
    tr = min(tr, seq)
    n_out = mu.shape[0]
    row = pl.BlockSpec((tr, d), lambda i: (i, 0))
    return pl.pallas_call(
        functools.partial(_shift_kernel, tiles_per_seq=seq // tr),
        grid=(m // tr,),
        in_specs=[row, pl.BlockSpec((8, d), lambda i: (jnp.maximum(i * (tr // 8) - 1, 0), 0)),
                  pl.BlockSpec((n_out, d), lambda i: (0, 0))],
        out_specs=[row] * n_out,
        out_shape=[jax.ShapeDtypeStruct((m, d), BF16)] * n_out,
        compiler_params=_params("parallel"),
        name="token_shift_mix",
    )(x, x, mu)


def _split_bf16(x, terms):
    parts = []
    for _ in range(terms - 1):
        p = x.astype(BF16)
        parts.append(p)
        x = x - p.astype(F32)
    parts.append(x.astype(BF16))
    return parts


def _rwkv_kernel(r_ref, k_ref, v_ref, wl_ref, al_ref, g_ref, w0_ref, a0_ref, kkw_ref, ka_ref, rk_ref,
                 gng_ref, gnb_ref, o_ref, s_ref, *, L, G, SG):
    N = C_HEAD_DIM
    P = 2 * N
    assert L == N and P == 128

    @pl.when(pl.program_id(2) == 0)
    def _():
        s_ref[...] = jnp.zeros_like(s_ref)

    r128 = lax.broadcasted_iota(jnp.int32, (P, P), 0)
    c128 = lax.broadcasted_iota(jnp.int32, (P, P), 1)
    same_head = (r128 // N) == (c128 // N)
    ones_blk = same_head.astype(BF16)
    t_i = lax.broadcasted_iota(jnp.int32, (L, P), 0)
    s_i = lax.broadcasted_iota(jnp.int32, (L, P), 1) % N
    strict = s_i < t_i
    incl = s_i <= t_i
    eye = (s_i == t_i).astype(F32)
    same_block = lambda width: (t_i // width) == (s_i // width)
    base = 8
    diag_base = same_block(base)
    merges = []
    width = base
    while width < L:
        merges.append(jnp.logical_and(same_block(2 * width), jnp.logical_not(same_block(width))))
        width *= 2
    tri = (lax.broadcasted_iota(jnp.int32, (L, L), 1) <= lax.broadcasted_iota(jnp.int32, (L, L), 0)).astype(BF16)
    lane_head0 = lax.broadcasted_iota(jnp.int32, (1, P), 1) < N
    nt = (((1,), (1,)), ((), ()))

    def seg_sum(x):
        return jnp.dot(x.astype(BF16), ones_blk, preferred_element_type=F32)

    def block_diag(x_cat):
        return jnp.where(same_head, jnp.concatenate([x_cat, x_cat], axis=0), 0.0).astype(BF16)

    def mm(a_cat, b_cat):
        return jnp.dot(a_cat.astype(BF16), block_diag(b_cat), preferred_element_type=F32)

    def lanes(g):
        return pl.ds(g * P, P) if isinstance(g, int) else pl.ds(pl.multiple_of(g * P, P), P)

    def chunk(sl, s_blk):
        r = r_ref[:, sl].astype(F32)
        k = k_ref[:, sl]
        v = v_ref[:, sl].astype(F32)
        wraw = wl_ref[:, sl] + w0_ref[:, sl]
        w_log = -(jnp.maximum(-wraw, 0.0) + _softplus_neg_abs(wraw)) - 0.5
        ld = -jnp.exp(w_log)
        cl2 = jnp.dot(tri, jnp.concatenate(_split_bf16(ld, 2), axis=1), preferred_element_type=F32)
        yield
        a = _sigmoid(al_ref[:, sl].astype(F32) + a0_ref[:, sl])
        kkr = k * kkw_ref[:, sl]
        kk_ss = seg_sum(kkr * kkr)
        yield
        kk = kkr * lax.rsqrt(jnp.maximum(kk_ss, 1e-24))
        kh = k * (1.0 + (a - 1.0) * ka_ref[:, sl])
        b = -(kk * a)
        bonus = seg_sum(r * kh * rk_ref[:, sl])
        yield
        cl = cl2[:, :P] + cl2[:, P:]
        cl_last = cl[L - 1:L, :]
        e_inv = jnp.exp(-cl)
        gam = jnp.exp(cl_last)
        e_tail = gam * e_inv
        lhs = jnp.concatenate([kk * jnp.exp(cl - ld), r * jnp.exp(cl)], axis=0)
        rhs = jnp.concatenate([b * e_inv, kh * e_inv], axis=0).astype(BF16)
        res0 = lax.dot_general(jnp.where(lane_head0, lhs, 0.0).astype(BF16), rhs, nt, preferred_element_type=F32)
        yield
        res1 = lax.dot_general(jnp.where(lane_head0, 0.0, lhs).astype(BF16), rhs, nt, preferred_element_type=F32)
        yield
        ls = lax.dot_general(lhs.astype(BF16), s_blk.astype(BF16), nt, preferred_element_type=F32)
        yield
        pair = lambda blk_r, blk_c: jnp.concatenate(
            [res0[blk_r * L:(blk_r + 1) * L, blk_c * L:(blk_c + 1) * L],
             res1[blk_r * L:(blk_r + 1) * L, blk_c * L:(blk_c + 1) * L]], axis=1)
        nb = jnp.where(strict, pair(0, 0), 0.0)
        nk = jnp.where(strict, pair(0, 1), 0.0)
        mb = jnp.where(incl, pair(1, 0), 0.0)
        mk = jnp.where(incl, pair(1, 1), 0.0)
        v_blk = block_diag(v)
        rhs_rows = ls[:L] + jnp.dot(nk.astype(BF16), v_blk, preferred_element_type=F32)
        yield

        nd = jnp.where(diag_base, nb, 0.0)
        n2 = mm(nd, nd)
        yield
        n4 = mm(n2, n2)
        yield
        t_inv = eye + nd + n2 + mm(nd, n2)
        yield
        t_inv = t_inv + mm(t_inv, n4)
        yield
        for joined in merges:
            half = mm(t_inv, jnp.where(joined, nb, 0.0))
            yield
            t_inv = t_inv + mm(half, t_inv)
            yield
        sk = mm(t_inv, rhs_rows)
        yield

        y = ls[L:] + jnp.dot(jnp.concatenate([mb, mk], axis=1).astype(BF16),
                             jnp.concatenate([block_diag(sk), v_blk], axis=0), preferred_element_type=F32)
        yield
        d_s = lax.dot_general(jnp.concatenate([sk, v], axis=0).astype(BF16),
                              jnp.concatenate([b * e_tail, kh * e_tail], axis=0).astype(BF16),
                              (((0,), (0,)), ((), ())), preferred_element_type=F32)
        yield
        s_new = gam * s_blk + jnp.where(same_head, d_s, 0.0)

        mean = seg_sum(y) * (1.0 / N)
        yield
        yc = y - mean
        var = seg_sum(yc * yc) * (1.0 / N)
        yield
        yn = yc * lax.rsqrt(var + C_GN_EPS) * gng_ref[:, sl] + gnb_ref[:, sl]
        return s_new, ((yn + bonus * v) * g_ref[:, sl].astype(F32)).astype(o_ref.dtype)

    def sub_group(first):
        pairs = [first + g for g in range(SG)]
        results = _round_robin([chunk(lanes(p), s_ref[p]) for p in pairs])
        for p, (s_new, out) in zip(pairs, results):
            s_ref[p] = s_new
            o_ref[:, lanes(p)] = out

    if G == SG:
        sub_group(0)
    else:
        def body(sg, carry):
            sub_group(sg * SG)
            return carry

        lax.fori_loop(0, G // SG, body, 0)


def rwkv7_scan(r, k, v, wl, al, g, w0, a0, k_k, k_a, r_k, gn_g, gn_b, *, bsz, seq, G=32, SG=16):
    m, d = r.shape
    L = C_HEAD_DIM
    P = 2 * C_HEAD_DIM
    nc = seq // L
    wide = G * P
    tile = pl.BlockSpec((L, wide), lambda b, j, c: (b * nc + c, j))
    vec = pl.BlockSpec((1, wide), lambda b, j, c: (0, j))
    vecs = [t.reshape(1, d).astype(F32) for t in (w0, a0, k_k, k_a, r_k, gn_g, gn_b)]
    return pl.pallas_call(
        functools.partial(_rwkv_kernel, L=L, G=G, SG=SG),
        grid=(bsz, d // wide, nc),
        in_specs=[tile] * 6 + [vec] * 7,
        out_specs=tile,
        out_shape=jax.ShapeDtypeStruct((m, d), BF16),
        scratch_shapes=[pltpu.VMEM((G, P, P), F32)],
        compiler_params=_params("parallel", "parallel", "arbitrary"),
        name="rwkv7_scan",
    )(r, k, v, wl, al, g, *vecs)


def rwkv7_mixer(x, mu, w_rkv, w0, w1, w2, a0, a1, a2, g1, g2, k_k, k_a, r_k, gn_g, gn_b, w_out, idx, *, bsz, seq):
    xr, xw, xk, xv, xa, xg = token_shift_mix(x, mu, seq=seq)
    r = matmul(xr, w_rkv, lead=(idx, 0), out_dtype=BF16, name="rwkv_r")
    k = matmul(xk, w_rkv, lead=(idx, 1), name="rwkv_k")
    v = matmul(xv, w_rkv, lead=(idx, 2), out_dtype=BF16, name="rwkv_v")
    wl = matmul(matmul(xw, w1.astype(BF16), act="tanh", out_dtype=BF16, name="rwkv_w1"), w2.astype(BF16),
                name="rwkv_w2")
    al = matmul(matmul(xa, a1.astype(BF16), out_dtype=BF16, name="rwkv_a1"), a2.astype(BF16), out_dtype=BF16,
                name="rwkv_a2")
    n_g = g1.shape[1]
    n_gp = -(-n_g // 128) * 128
    g1p = jnp.zeros((g1.shape[0], n_gp), BF16).at[:, :n_g].set(g1.astype(BF16))
    g2p = jnp.zeros((n_gp, g2.shape[1]), BF16).at[:n_g, :].set(g2.astype(BF16))
    g = matmul(matmul(xg, g1p, act="sigmoid", out_dtype=BF16, name="rwkv_g1"), g2p, out_dtype=BF16, name="rwkv_g2")
    y = rwkv7_scan(r, k, v, wl, al, g, w0, a0, k_k, k_a, r_k, gn_g, gn_b, bsz=bsz, seq=seq)
    return matmul(y, w_out, lead=(idx,), out_dtype=BF16, name="rwkv_out")


def kernel(x, norm_g, mlp_w1, mlp_w2, a_w_in, a_b_if, a_norm_g, a_w_out, b_w_in, b_w_out, c_mu, c_w_rkv,
           c_w0, c_w1, c_w2, c_a0, c_a1, c_a2, c_g1, c_g2, c_k_k, c_k_a, c_r_k, c_gn_g, c_gn_b, c_w_out):
    bsz, seq, d = x.shape
    depth = norm_g.shape[0]
    h = x.reshape(bsz * seq, d)
    kinds = [layer % 3 for layer in range(depth)]
    pre_dtype = lambda kind: F32 if kind == 2 else BF16
    y = rms_norm(h, norm_g[0, 0], out_dtype=pre_dtype(kinds[0]))
    mlp_w2_bf16 = mlp_w2.astype(BF16)
    for layer in range(depth):
        kind, idx = kinds[layer], layer // 3
        if kind == 0:
            u = mlstm_mixer(y, a_w_in, a_b_if, a_norm_g, a_w_out, idx, bsz=bsz, seq=seq)
        elif kind == 1:
            u = stick_breaking_mixer(y, b_w_in, b_w_out, idx, bsz=bsz, seq=seq)
        else:
            u = rwkv7_mixer(y, c_mu[idx], c_w_rkv, c_w0[idx], c_w1[idx], c_w2[idx], c_a0[idx], c_a1[idx],
                            c_a2[idx], c_g1[idx], c_g2[idx], c_k_k[idx], c_k_a[idx], c_r_k[idx], c_gn_g[idx],
                            c_gn_b[idx], c_w_out, idx, bsz=bsz, seq=seq)
        h, y = add_rms(h, u, norm_g[layer, 1], norm_g[layer, 2])
        mid = matmul(y, mlp_w1, lead=(layer,), act="relu2", out_dtype=BF16, name="mlp_up")
        u = matmul(mid, mlp_w2_bf16, lead=(layer,), out_dtype=BF16, name="mlp_down")
        last = layer == depth - 1
        h, y = add_rms(h, u, norm_g[layer, 3], None if last else norm_g[layer + 1, 0],
                       pre_dtype=BF16 if last else pre_dtype(kinds[layer + 1]))
    return h.reshape(bsz, seq, d)
```

```python
import functools

import jax
import jax.numpy as jnp
from jax import lax
from jax.experimental import pallas as pl
from jax.experimental.pallas import tpu as pltpu

F32 = jnp.float32
BF16 = jnp.bfloat16

EPS = 1e-6
VMEM_LIMIT_BYTES = 56 * 1024 * 1024
MM_TILE_BYTES = 8 * 1024 * 1024

A_HEADS = 8
A_GATE_CAP = 15.0
B_HEADS = 32
Q_BLOCK = 128
C_HEAD_DIM = 64
C_GN_EPS = 64e-5
SB_UNDERFLOW = -104.0


def _params(*sem):
    return pltpu.CompilerParams(dimension_semantics=sem, vmem_limit_bytes=VMEM_LIMIT_BYTES)


def _softplus_neg_abs(z):
    return jnp.log(1.0 + jnp.exp(-jnp.abs(z)))


def _log_sigmoid(z):
    return jnp.minimum(z, 0.0) - _softplus_neg_abs(z)


def _sigmoid(z):
    return 1.0 / (1.0 + jnp.exp(-z))


def _round_robin(gens):
    results = [None] * len(gens)
    active = list(enumerate(gens))
    while active:
        still = []
        for i, gen in active:
            try:
                next(gen)
                still.append((i, gen))
            except StopIteration as stop:
                results[i] = stop.value
        active = still
    return results


_ACTS = {
    "none": lambda r: r,
    "relu2": lambda r: jnp.square(jnp.maximum(r, 0.0)),
    "sigmoid": _sigmoid,
    "tanh": jnp.tanh,
}


def _mm_kernel(a_ref, w_ref, o_ref, *scratch, act, nk, w_t):
    w_contract = 1 if w_t else 0
    part = lax.dot_general(a_ref[...].astype(BF16), w_ref[...].astype(BF16), (((1,), (w_contract,)), ((), ())),
                           preferred_element_type=F32)
    if nk == 1:
        o_ref[...] = _ACTS[act](part).astype(o_ref.dtype)
        return
    acc_ref, = scratch
    k = pl.program_id(2)

    @pl.when(k == 0)
    def _():
        acc_ref[...] = part

    @pl.when(jnp.logical_and(k > 0, k < nk - 1))
    def _():
        acc_ref[...] += part

    @pl.when(k == nk - 1)
    def _():
        o_ref[...] = _ACTS[act](acc_ref[...] + part).astype(o_ref.dtype)


def matmul(a, w, *, lead=(), n=None, w_t=False, act="none", out_dtype=F32, tm=1024, tn=None, tk=4096,
           name="matmul"):
    m, k = a.shape
    k2, n_w = w.shape[len(lead):][::-1] if w_t else w.shape[len(lead):]
    n = n_w if n is None else n
    assert k == k2 and n <= n_w and w.ndim == len(lead) + 2
    if tn is None:
        tn = min(1024, MM_TILE_BYTES // (min(tk, k) * w.dtype.itemsize))
    tm, tn = min(tm, m), min(tn, n)
    tk = min(tk, k, MM_TILE_BYTES // (tm * a.dtype.itemsize))
    assert m % tm == 0 and n % tn == 0 and k % tk == 0, (a.shape, w.shape)
    nk = k // tk
    if w_t:
        w_spec = pl.BlockSpec((None,) * len(lead) + (tn, tk), lambda i, j, kk: (*lead, j, kk))
    else:
        w_spec = pl.BlockSpec((None,) * len(lead) + (tk, tn), lambda i, j, kk: (*lead, kk, j))
    return pl.pallas_call(
        functools.partial(_mm_kernel, act=act, nk=nk, w_t=w_t),
        grid=(m // tm, n // tn, nk),
        in_specs=[pl.BlockSpec((tm, tk), lambda i, j, kk: (i, kk)), w_spec],
        out_specs=pl.BlockSpec((tm, tn), lambda i, j, kk: (i, j)),
        out_shape=jax.ShapeDtypeStruct((m, n), out_dtype),
        scratch_shapes=[pltpu.VMEM((tm, tn), F32)] if nk > 1 else [],
        compiler_params=_params("parallel", "parallel", "arbitrary"),
        name=name,
    )(a, w)


def _mm_rowacc_kernel(a_ref, w_ref, o_ref, acc_ref, *, nk):
    k, j = pl.program_id(1), pl.program_id(2)
    part = jnp.dot(a_ref[...].astype(BF16), w_ref[...].astype(BF16), preferred_element_type=F32)

    @pl.when(k == 0)
    def _():
        acc_ref[j] = part
        o_ref[...] = part.astype(o_ref.dtype)

    @pl.when(k > 0)
    def _():
        total = acc_ref[j] + part

        @pl.when(k < nk - 1)
        def _():
            acc_ref[j] = total

        o_ref[...] = total.astype(o_ref.dtype)


def matmul_rowacc(a, w, *, lead=(), out_dtype=F32, tm=1024, tn=512, tk=4096, name="matmul_rowacc"):
    m, k = a.shape
    k2, n = w.shape[len(lead):]
    assert k == k2 and m % tm == 0 and n % tn == 0 and k % tk == 0 and k // tk > 1
    nk = k // tk
    return pl.pallas_call(
        functools.partial(_mm_rowacc_kernel, nk=nk),
        grid=(m // tm, nk, n // tn),
        in_specs=[pl.BlockSpec((tm, tk), lambda i, kk, j: (i, kk)),
                  pl.BlockSpec((None,) * len(lead) + (tk, tn), lambda i, kk, j: (*lead, kk, j))],
        out_specs=pl.BlockSpec((tm, tn), lambda i, kk, j: (i, j)),
        out_shape=jax.ShapeDtypeStruct((m, n), out_dtype),
        scratch_shapes=[pltpu.VMEM((n // tn, tm, tn), F32)],
        compiler_params=_params("parallel", "arbitrary", "arbitrary"),
        name=name,
    )(a, w)


def _rms(x, g):
    return x * lax.rsqrt(jnp.mean(x * x, axis=-1, keepdims=True) + EPS) * g


def _rms_kernel(x_ref, g_ref, o_ref):
    o_ref[...] = _rms(x_ref[...], g_ref[...]).astype(o_ref.dtype)


def rms_norm(x, g, *, out_dtype=BF16, tr=256):
    m, d = x.shape
    tr = min(tr, m)
    return pl.pallas_call(
        _rms_kernel,
        grid=(m // tr,),
        in_specs=[pl.BlockSpec((tr, d), lambda i: (i, 0)), pl.BlockSpec((1, d), lambda i: (0, 0))],
        out_specs=pl.BlockSpec((tr, d), lambda i: (i, 0)),
        out_shape=jax.ShapeDtypeStruct((m, d), out_dtype),
        compiler_params=_params("parallel"),
        name="rms_norm",
    )(x, g.reshape(1, d))


def _add_rms_kernel(h_ref, u_ref, gpost_ref, *rest, with_pre):
    hn = h_ref[...] + _rms(u_ref[...].astype(F32), gpost_ref[...])
    if with_pre:
        gpre_ref, hn_ref, y_ref = rest
        y_ref[...] = _rms(hn, gpre_ref[...]).astype(y_ref.dtype)
    else:
        hn_ref, = rest
    hn_ref[...] = hn


def add_rms(h, u, g_post, g_pre=None, *, pre_dtype=BF16, tr=128):
    m, d = h.shape
    tr = min(tr, m)
    with_pre = g_pre is not None
    row = pl.BlockSpec((tr, d), lambda i: (i, 0))
    vec = pl.BlockSpec((1, d), lambda i: (0, 0))
    args = [h, u, g_post.reshape(1, d)] + ([g_pre.reshape(1, d)] if with_pre else [])
    out_shape = [jax.ShapeDtypeStruct((m, d), F32)] + ([jax.ShapeDtypeStruct((m, d), pre_dtype)] if with_pre else [])
    outs = pl.pallas_call(
        functools.partial(_add_rms_kernel, with_pre=with_pre),
        grid=(m // tr,),
        in_specs=[row, row, vec] + ([vec] if with_pre else []),
        out_specs=[row] * len(out_shape),
        out_shape=out_shape,
        compiler_params=_params("parallel"),
        name="add_rms",
    )(*args)
    return outs if with_pre else (outs[0], None)


def _mlstm_gate_kernel(x_ref, w_ref, b_ref, o_ref):
    row_w = lax.broadcasted_iota(jnp.int32, w_ref.shape, 0)
    w = jnp.where(row_w < 2 * A_HEADS, w_ref[...], 0.0).astype(BF16)
    g = lax.dot_general(x_ref[...], w, (((1,), (1,)), ((), ())), preferred_element_type=F32) + b_ref[...]
    lane = lax.broadcasted_iota(jnp.int32, g.shape, 1)
    ig = A_GATE_CAP * jnp.tanh(g / A_GATE_CAP)
    o_ref[...] = jnp.where(lane < A_HEADS, ig, _log_sigmoid(g))


def mlstm_gates(x, w_in_t, b_if, idx, *, tr=512):
    m, d = x.shape
    tr = min(tr, m)
    ng = 128
    n_main = w_in_t.shape[1] - 2 * A_HEADS
    assert n_main % ng == 0
    b = jnp.zeros((1, ng), F32).at[0, :2 * A_HEADS].set(b_if[idx].astype(F32))
    return pl.pallas_call(
        _mlstm_gate_kernel,
        grid=(m // tr,),
        in_specs=[pl.BlockSpec((tr, d), lambda i: (i, 0)),
                  pl.BlockSpec((None, ng, d), lambda i: (idx, n_main // ng, 0)),
                  pl.BlockSpec((1, ng), lambda i: (0, 0))],
        out_specs=pl.BlockSpec((tr, ng), lambda i: (i, 0)),
        out_shape=jax.ShapeDtypeStruct((m, ng), F32),
        compiler_params=_params("parallel"),
        name="mlstm_gates",
    )(x, w_in_t, b)


def _mlstm_kernel(q_ref, k_ref, v_ref, og_ref, igr_ref, lfr_ref, g_ref,
                  out_ref, c_ref, n_ref, m_ref, *, L, hg, qk_scale):
    @pl.when(pl.program_id(2) == 0)
    def _():
        c_ref[...] = jnp.zeros_like(c_ref)
        n_ref[...] = jnp.zeros_like(n_ref)
        m_ref[...] = jnp.zeros_like(m_ref)

    dqk = q_ref.shape[1] // hg
    dv = v_ref.shape[1] // hg
    row = lax.broadcasted_iota(jnp.int32, (L, L), 0)
    col = lax.broadcasted_iota(jnp.int32, (L, L), 1)
    causal = col <= row
    tri = causal.astype(F32)
    tri_t = (row <= col).astype(F32)
    eye = row == col

    def head(h):
        qk_cols = slice(h * dqk, (h + 1) * dqk)
        v_cols = slice(h * dv, (h + 1) * dv)
        q = q_ref[:, qk_cols] * qk_scale
        k = k_ref[:, qk_cols]
        v = v_ref[:, v_cols]
        ig_row, lf_row = igr_ref[h], lfr_ref[h]
        ig_col = jnp.sum(jnp.where(eye, ig_row, 0.0), axis=1, keepdims=True)
        lf_col = jnp.sum(jnp.where(eye, lf_row, 0.0), axis=1, keepdims=True)
        m_prev = m_ref[h]

        b_col = jnp.sum(tri * lf_row, axis=1, keepdims=True)
        b_row = jnp.sum(tri_t * lf_col, axis=0, keepdims=True)
        b_last = jnp.sum(lf_row, axis=1, keepdims=True)

        d = jnp.where(causal, b_col - b_row + ig_row, -jnp.inf)
        inter = b_col + m_prev
        m_t = jnp.maximum(jnp.max(d, axis=1, keepdims=True), inter)
        qk = lax.dot_general(q, k, (((1,), (1,)), ((), ())), preferred_element_type=F32)
        yield
        s = qk * jnp.exp(d - m_t)
        inter_w = jnp.exp(inter - m_t)
        c_prev = c_ref[h]
        n_prev = n_ref[h]
        sv = jnp.dot(s.astype(BF16), v, preferred_element_type=F32)
        yield
        qc = jnp.dot(q, c_prev.astype(BF16), preferred_element_type=F32)
        yield
        num = sv + inter_w * qc
        qn = jnp.sum(q.astype(F32) * n_prev, axis=1, keepdims=True)
        den = jnp.sum(s, axis=1, keepdims=True) + inter_w * qn
        hid = num / jnp.maximum(jnp.abs(den), jnp.exp(-m_t))

        a_col = b_last - b_col + ig_col
        a_row = b_last - b_row + ig_row
        m_new = jnp.maximum(b_last + m_prev, jnp.max(a_row, axis=1, keepdims=True))
        w_col = jnp.exp(a_col - m_new)
        decay = jnp.exp(b_last + m_prev - m_new)
        kw = k.astype(F32) * w_col
        kv = lax.dot_general(kw.astype(BF16), v, (((0,), (0,)), ((), ())), preferred_element_type=F32)
        yield
        c_new = decay * c_prev + kv
        n_new = decay * n_prev + jnp.sum(kw, axis=0, keepdims=True)

        hn = hid * lax.rsqrt(jnp.mean(hid * hid, axis=1, keepdims=True) + EPS) * g_ref[:, v_cols]
        out = (hn * _sigmoid(og_ref[:, v_cols].astype(F32))).astype(out_ref.dtype)
        return c_new, n_new, m_new, out

    for h, (c_new, n_new, m_new, out) in enumerate(_round_robin([head(h) for h in range(hg)])):
        c_ref[h] = c_new
        n_ref[h] = n_new
        m_ref[h] = m_new
        out_ref[:, h * dv:(h + 1) * dv] = out


def mlstm_scan(proj, gates, norm_g, *, bsz, seq, L=128, hg=8):
    h_ = A_HEADS
    d_model = norm_g.shape[0]
    dv = d_model // h_
    dqk = dv // 2
    nc = seq // L
    ng = h_ // hg
    g4 = gates[:, :2 * h_].reshape(bsz, nc, L, 2 * h_).transpose(0, 3, 1, 2)
    ig, lf = g4[:, :h_], g4[:, h_:]
    rowv = lambda t: t[..., None, :]
    row_spec = pl.BlockSpec((None, hg, None, 1, L), lambda b, h, c: (b, h, c, 0, 0))
    kq = 2 * h_ * dqk // dv
    return pl.pallas_call(
        functools.partial(_mlstm_kernel, L=L, hg=hg, qk_scale=dqk ** -0.5),
        grid=(bsz, ng, nc),
        in_specs=[pl.BlockSpec((L, hg * dqk), lambda b, h, c: (b * nc + c, h)),
                  pl.BlockSpec((L, hg * dqk), lambda b, h, c: (b * nc + c, ng + h)),
                  pl.BlockSpec((L, hg * dv), lambda b, h, c: (b * nc + c, kq // hg + h)),
                  pl.BlockSpec((L, hg * dv), lambda b, h, c: (b * nc + c, (kq + h_) // hg + h)),
                  row_spec, row_spec,
                  pl.BlockSpec((1, hg * dv), lambda b, h, c: (0, h))],
        out_specs=pl.BlockSpec((L, hg * dv), lambda b, h, c: (b * nc + c, h)),
        out_shape=jax.ShapeDtypeStruct((bsz * seq, d_model), BF16),
        scratch_shapes=[pltpu.VMEM((hg, dqk, dv), F32), pltpu.VMEM((hg, 1, dqk), F32),
                        pltpu.VMEM((hg, 1, 1), F32)],
        compiler_params=_params("parallel", "parallel", "arbitrary"),
        name="mlstm_scan",
    )(proj, proj, proj, proj, rowv(ig), rowv(lf), norm_g.reshape(1, d_model).astype(F32))


def mlstm_mixer(x, w_in, b_if, norm_g, w_out, idx, *, bsz, seq):
    n_main = w_in.shape[2] - 2 * A_HEADS
    w_in_t = jnp.swapaxes(w_in, 1, 2)
    proj = matmul(x, w_in_t, lead=(idx,), n=n_main, w_t=True, out_dtype=BF16, name="mlstm_in")
    gates = mlstm_gates(x, w_in_t, b_if, idx)
    hg = mlstm_scan(proj, gates, norm_g[idx], bsz=bsz, seq=seq)
    return matmul(hg, w_out, lead=(idx,), out_dtype=BF16, name="mlstm_out")


def _sb_kernel(q_ref, k_ref, v_ref, o_ref, *, scale, hg):
    T = Q_BLOCK
    dh = q_ref.shape[1] // hg
    qi = pl.program_id(2)
    row = lax.broadcasted_iota(jnp.int32, (T, T), 0)
    col = lax.broadcasted_iota(jnp.int32, (T, T), 1)
    strict = col < row
    upper = (row > col).astype(BF16)
    upper2 = jnp.concatenate([upper, upper], axis=0)

    def head_block(q, k, v, carry, acc, masked):
        z = lax.dot_general(q, k, (((1,), (1,)), ((), ())), preferred_element_type=F32) * scale
        yield
        sp = _softplus_neg_abs(z)
        ls_pos = jnp.minimum(z, 0.0) - sp
        lk = jnp.minimum(-z, 0.0) - sp
        if masked:
            lk = jnp.where(strict, lk, 0.0)
        hi = lk.astype(BF16)
        lo = (lk - hi.astype(F32)).astype(BF16)
        after = jnp.dot(jnp.concatenate([hi, lo], axis=1), upper2, preferred_element_type=F32)
        yield
        w = jnp.exp(ls_pos + after + carry)
        if masked:
            w = jnp.where(strict, w, 0.0)
        acc = acc + jnp.dot(w.astype(BF16), v, preferred_element_type=F32)
        carry = carry + jnp.sum(lk, axis=1, keepdims=True)
        return carry, acc

    heads = [slice(h * dh, (h + 1) * dh) for h in range(hg)]
    qs = [q_ref[:, sl] for sl in heads]

    def block(kb, carries, accs, masked):
        start = pl.multiple_of(kb * T, T)
        out = _round_robin([head_block(qs[h], k_ref[pl.ds(start, T), heads[h]], v_ref[pl.ds(start, T), heads[h]],
                                       carries[h], accs[h], masked) for h in range(hg)])
        return tuple(o[0] for o in out), tuple(o[1] for o in out)

    carries, accs = block(qi, (jnp.zeros((T, 1), F32),) * hg, (jnp.zeros((T, dh), F32),) * hg, True)

    def live(carries):
        worst = functools.reduce(jnp.maximum, carries)
        return jnp.max(worst) >= SB_UNDERFLOW

    def cond(state):
        j, alive, _, _ = state
        return jnp.logical_and(j < qi, alive)

    def body(state):
        j, _, carries, accs = state
        carries, accs = block(qi - 1 - j, carries, accs, False)
        return j + 1, live(carries), carries, accs

    _, _, _, accs = lax.while_loop(cond, body, (jnp.int32(0), live(carries), carries, accs))
    for h in range(hg):
        o_ref[:, heads[h]] = accs[h].astype(o_ref.dtype)


def stick_breaking_scan(qkv, *, bsz, seq, hg=8):
    d_model = qkv.shape[1] // 3
    dh = d_model // B_HEADS
    nq = seq // Q_BLOCK
    ng = B_HEADS // hg
    wide = hg * dh
    return pl.pallas_call(
        functools.partial(_sb_kernel, scale=dh ** -0.5, hg=hg),
        grid=(bsz, ng, nq),
        in_specs=[pl.BlockSpec((Q_BLOCK, wide), lambda b, h, i: (b * nq + i, h)),
                  pl.BlockSpec((seq, wide), lambda b, h, i: (b, ng + h)),
                  pl.BlockSpec((seq, wide), lambda b, h, i: (b, 2 * ng + h))],
        out_specs=pl.BlockSpec((Q_BLOCK, wide), lambda b, h, i: (b * nq + i, h)),
        out_shape=jax.ShapeDtypeStruct((bsz * seq, d_model), BF16),
        compiler_params=_params("parallel", "parallel", "arbitrary"),
        name="stick_breaking_scan",
    )(qkv, qkv, qkv)


def stick_breaking_mixer(x, w_in, w_out, idx, *, bsz, seq):
    qkv = matmul(x, w_in, lead=(idx,), out_dtype=BF16, name="sb_in")
    o = stick_breaking_scan(qkv, bsz=bsz, seq=seq)
    return matmul(o, w_out, lead=(idx,), out_dtype=BF16, name="sb_out")


def _shift_kernel(x_ref, prev_ref, mu_ref, *o_refs, tiles_per_seq):
    x = x_ref[...]
    first = pl.program_id(0) % tiles_per_seq == 0
    prev_last = jnp.where(first, 0.0, prev_ref[7:8, :])
    row = lax.broadcasted_iota(jnp.int32, x.shape, 0)
    x_prev = jnp.where(row == 0, prev_last, pltpu.roll(x, 1, 0))
    xx = x_prev - x
    for i, o_ref in enumerate(o_refs):
        o_ref[...] = (x + xx * mu_ref[i:i + 1, :]).astype(o_ref.dtype)


def token_shift_mix(x, mu, *, seq, tr=256):
    m, d = x.shape
    tr = min(tr, seq)
    n_out = mu.shape[0]
    row = pl.BlockSpec((tr, d), lambda i: (i, 0))
    return pl.pallas_call(
        functools.partial(_shift_kernel, tiles_per_seq=seq // tr),
        grid=(m // tr,),
        in_specs=[row, pl.BlockSpec((8, d), lambda i: (jnp.maximum(i * (tr // 8) - 1, 0), 0)),
                  pl.BlockSpec((n_out, d), lambda i: (0, 0))],
        out_specs=[row] * n_out,
        out_shape=[jax.ShapeDtypeStruct((m, d), BF16)] * n_out,
        compiler_params=_params("parallel"),
        name="token_shift_mix",
    )(x, x, mu)


def _split_bf16(x, terms):
    parts = []
    for _ in range(terms - 1):
        p = x.astype(BF16)
        parts.append(p)
        x = x - p.astype(F32)
    parts.append(x.astype(BF16))
    return parts


def _rwkv_kernel(r_ref, k_ref, v_ref, wl_ref, al_ref, g_ref, w0_ref, a0_ref, kkw_ref, ka_ref, rk_ref,
                 gng_ref, gnb_ref, o_ref, s_ref, *, L, G, SG):
    N = C_HEAD_DIM
    P = 2 * N
    assert L == N and P == 128

    @pl.when(pl.program_id(2) == 0)
    def _():
        s_ref[...] = jnp.zeros_like(s_ref)

    r128 = lax.broadcasted_iota(jnp.int32, (P, P), 0)
    c128 = lax.broadcasted_iota(jnp.int32, (P, P), 1)
    same_head = (r128 // N) == (c128 // N)
    ones_blk = same_head.astype(BF16)
    t_i = lax.broadcasted_iota(jnp.int32, (L, P), 0)
    s_i = lax.broadcasted_iota(jnp.int32, (L, P), 1) % N
    strict = s_i < t_i
    incl = s_i <= t_i
    eye = (s_i == t_i).astype(F32)
    same_block = lambda width: (t_i // width) == (s_i // width)
    base = 8
    diag_base = same_block(base)
    merges = []
    width = base
    while width < L:
        merges.append(jnp.logical_and(same_block(2 * width), jnp.logical_not(same_block(width))))
        width *= 2
    tri = (lax.broadcasted_iota(jnp.int32, (L, L), 1) <= lax.broadcasted_iota(jnp.int32, (L, L), 0)).astype(BF16)
    lane_head0 = lax.broadcasted_iota(jnp.int32, (1, P), 1) < N
    nt = (((1,), (1,)), ((), ()))

    def seg_sum(x):
        return jnp.dot(x.astype(BF16), ones_blk, preferred_element_type=F32)

    def block_diag(x_cat):
        return jnp.where(same_head, jnp.concatenate([x_cat, x_cat], axis=0), 0.0).astype(BF16)

    def mm(a_cat, b_cat):
        return jnp.dot(a_cat.astype(BF16), block_diag(b_cat), preferred_element_type=F32)

    def lanes(g):
        return pl.ds(g * P, P) if isinstance(g, int) else pl.ds(pl.multiple_of(g * P, P), P)

    def chunk(sl, s_blk):
        r = r_ref[:, sl].astype(F32)
        k = k_ref[:, sl]
        v = v_ref[:, sl].astype(F32)
        wraw = wl_ref[:, sl] + w0_ref[:, sl]
        w_log = -(jnp.maximum(-wraw, 0.0) + _softplus_neg_abs(wraw)) - 0.5
        ld = -jnp.exp(w_log)
        cl2 = jnp.dot(tri, jnp.concatenate(_split_bf16(ld, 2), axis=1), preferred_element_type=F32)
        yield
        a = _sigmoid(al_ref[:, sl].astype(F32) + a0_ref[:, sl])
        kkr = k * kkw_ref[:, sl]
        kk_ss = seg_sum(kkr * kkr)
        yield
        kk = kkr * lax.rsqrt(jnp.maximum(kk_ss, 1e-24))
        kh = k * (1.0 + (a - 1.0) * ka_ref[:, sl])
        b = -(kk * a)
        bonus = seg_sum(r * kh * rk_ref[:, sl])
        yield
        cl = cl2[:, :P] + cl2[:, P:]
        cl_last = cl[L - 1:L, :]
        e_inv = jnp.exp(-cl)
        gam = jnp.exp(cl_last)
        e_tail = gam * e_inv
        lhs = jnp.concatenate([kk * jnp.exp(cl - ld), r * jnp.exp(cl)], axis=0)
        rhs = jnp.concatenate([b * e_inv, kh * e_inv], axis=0).astype(BF16)
        res0 = lax.dot_general(jnp.where(lane_head0, lhs, 0.0).astype(BF16), rhs, nt, preferred_element_type=F32)
        yield
        res1 = lax.dot_general(jnp.where(lane_head0, 0.0, lhs).astype(BF16), rhs, nt, preferred_element_type=F32)
        yield
        ls = lax.dot_general(lhs.astype(BF16), s_blk.astype(BF16), nt, preferred_element_type=F32)
        yield
        pair = lambda blk_r, blk_c: jnp.concatenate(
            [res0[blk_r * L:(blk_r + 1) * L, blk_c * L:(blk_c + 1) * L],
             res1[blk_r * L:(blk_r + 1) * L, blk_c * L:(blk_c + 1) * L]], axis=1)
        nb = jnp.where(strict, pair(0, 0), 0.0)
        nk = jnp.where(strict, pair(0, 1), 0.0)
        mb = jnp.where(incl, pair(1, 0), 0.0)
        mk = jnp.where(incl, pair(1, 1), 0.0)
        v_blk = block_diag(v)
        rhs_rows = ls[:L] + jnp.dot(nk.astype(BF16), v_blk, preferred_element_type=F32)
        yield

        nd = jnp.where(diag_base, nb, 0.0)
        n2 = mm(nd, nd)
        yield
        n4 = mm(n2, n2)
        yield
        t_inv = eye + nd + n2 + mm(nd, n2)
        yield
        t_inv = t_inv + mm(t_inv, n4)
        yield
        for joined in merges:
            half = mm(t_inv, jnp.where(joined, nb, 0.0))
            yield
            t_inv = t_inv + mm(half, t_inv)
            yield
        sk = mm(t_inv, rhs_rows)
        yield

        y = ls[L:] + jnp.dot(jnp.concatenate([mb, mk], axis=1).astype(BF16),
                             jnp.concatenate([block_diag(sk), v_blk], axis=0), preferred_element_type=F32)
        yield
        d_s = lax.dot_general(jnp.concatenate([sk, v], axis=0).astype(BF16),
                              jnp.concatenate([b * e_tail, kh * e_tail], axis=0).astype(BF16),
                              (((0,), (0,)), ((), ())), preferred_element_type=F32)
        yield
        s_new = gam * s_blk + jnp.where(same_head, d_s, 0.0)

        mean = seg_sum(y) * (1.0 / N)
        yield
        yc = y - mean
        var = seg_sum(yc * yc) * (1.0 / N)
        yield
        yn = yc * lax.rsqrt(var + C_GN_EPS) * gng_ref[:, sl] + gnb_ref[:, sl]
        return s_new, ((yn + bonus * v) * g_ref[:, sl].astype(F32)).astype(o_ref.dtype)

    def sub_group(first):
        pairs = [first + g for g in range(SG)]
        results = _round_robin([chunk(lanes(p), s_ref[p]) for p in pairs])
        for p, (s_new, out) in zip(pairs, results):
            s_ref[p] = s_new
            o_ref[:, lanes(p)] = out

    if G == SG:
        sub_group(0)
    else:
        def body(sg, carry):
            sub_group(sg * SG)
            return carry

        lax.fori_loop(0, G // SG, body, 0)


def rwkv7_scan(r, k, v, wl, al, g, w0, a0, k_k, k_a, r_k, gn_g, gn_b, *, bsz, seq, G=32, SG=16):
    m, d = r.shape
    L = C_HEAD_DIM
    P = 2 * C_HEAD_DIM
    nc = seq // L
    wide = G * P
    tile = pl.BlockSpec((L, wide), lambda b, j, c: (b * nc + c, j))
    vec = pl.BlockSpec((1, wide), lambda b, j, c: (0, j))
    vecs = [t.reshape(1, d).astype(F32) for t in (w0, a0, k_k, k_a, r_k, gn_g, gn_b)]
    return pl.pallas_call(
        functools.partial(_rwkv_kernel, L=L, G=G, SG=SG),
        grid=(bsz, d // wide, nc),
        in_specs=[tile] * 6 + [vec] * 7,
        out_specs=tile,
        out_shape=jax.ShapeDtypeStruct((m, d), BF16),
        scratch_shapes=[pltpu.VMEM((G, P, P), F32)],
        compiler_params=_params("parallel", "parallel", "arbitrary"),
        name="rwkv7_scan",
    )(r, k, v, wl, al, g, *vecs)


def rwkv7_mixer(x, mu, w_rkv, w0, w1, w2, a0, a1, a2, g1, g2, k_k, k_a, r_k, gn_g, gn_b, w_out, idx, *, bsz, seq):
    xr, xw, xk, xv, xa, xg = token_shift_mix(x, mu, seq=seq)
    r = matmul(xr, w_rkv, lead=(idx, 0), out_dtype=BF16, name="rwkv_r")
    k = matmul(xk, w_rkv, lead=(idx, 1), name="rwkv_k")
    v = matmul(xv, w_rkv, lead=(idx, 2), out_dtype=BF16, name="rwkv_v")
    wl = matmul(matmul(xw, w1.astype(BF16), act="tanh", out_dtype=BF16, name="rwkv_w1"), w2.astype(BF16),
                name="rwkv_w2")
    al = matmul(matmul(xa, a1.astype(BF16), out_dtype=BF16, name="rwkv_a1"), a2.astype(BF16), out_dtype=BF16,
                name="rwkv_a2")
    n_g = g1.shape[1]
    n_gp = -(-n_g // 128) * 128
    g1p = jnp.zeros((g1.shape[0], n_gp), BF16).at[:, :n_g].set(g1.astype(BF16))
    g2p = jnp.zeros((n_gp, g2.shape[1]), BF16).at[:n_g, :].set(g2.astype(BF16))
    g = matmul(matmul(xg, g1p, act="sigmoid", out_dtype=BF16, name="rwkv_g1"), g2p, out_dtype=BF16, name="rwkv_g2")
    y = rwkv7_scan(r, k, v, wl, al, g, w0, a0, k_k, k_a, r_k, gn_g, gn_b, bsz=bsz, seq=seq)
    return matmul(y, w_out, lead=(idx,), out_dtype=BF16, name="rwkv_out")


def kernel(x, norm_g, mlp_w1, mlp_w2, a_w_in, a_b_if, a_norm_g, a_w_out, b_w_in, b_w_out, c_mu, c_w_rkv,
           c_w0, c_w1, c_w2, c_a0, c_a1, c_a2, c_g1, c_g2, c_k_k, c_k_a, c_r_k, c_gn_g, c_gn_b, c_w_out):
    bsz, seq, d = x.shape
    depth = norm_g.shape[0]
    h = x.reshape(bsz * seq, d)
    kinds = [layer % 3 for layer in range(depth)]
    pre_dtype = lambda kind: F32 if kind == 2 else BF16
    y = rms_norm(h, norm_g[0, 0], out_dtype=pre_dtype(kinds[0]))
    for layer in range(depth):
        kind, idx = kinds[layer], layer // 3
        if kind == 0:
            u = mlstm_mixer(y, a_w_in, a_b_if, a_norm_g, a_w_out, idx, bsz=bsz, seq=seq)
        elif kind == 1:
            u = stick_breaking_mixer(y, b_w_in, b_w_out, idx, bsz=bsz, seq=seq)
        else:
            u = rwkv7_mixer(y, c_mu[idx], c_w_rkv, c_w0[idx], c_w1[idx], c_w2[idx], c_a0[idx], c_a1[idx],
                            c_a2[idx], c_g1[idx], c_g2[idx], c_k_k[idx], c_k_a[idx], c_r_k[idx], c_gn_g[idx],
                            c_gn_b[idx], c_w_out, idx, bsz=bsz, seq=seq)
        h, y = add_rms(h, u, norm_g[layer, 1], norm_g[layer, 2])
        mid = matmul(y, mlp_w1, lead=(layer,), act="relu2", out_dtype=BF16, name="mlp_up")
        u = matmul_rowacc(mid, mlp_w2, lead=(layer,), out_dtype=BF16, name="mlp_down")
        last = layer == depth - 1
        h, y = add_rms(h, u, norm_g[layer, 3], None if last else norm_g[layer + 1, 0],
                       pre_dtype=BF16 if last else pre_dtype(kinds[layer + 1]))
    return h.reshape(bsz, seq, d)
```

```python
import functools

import jax
import jax.numpy as jnp
from jax import lax
from jax.experimental import pallas as pl
from jax.experimental.pallas import tpu as pltpu

F32 = jnp.float32
BF16 = jnp.bfloat16

EPS = 1e-6
VMEM_LIMIT_BYTES = 56 * 1024 * 1024
MM_TILE_BYTES = 8 * 1024 * 1024

A_HEADS = 8
A_GATE_CAP = 15.0
B_HEADS = 32
Q_BLOCK = 128
C_HEAD_DIM = 64
C_GN_EPS = 64e-5
SB_UNDERFLOW = -104.0


def _params(*sem):
    return pltpu.CompilerParams(dimension_semantics=sem, vmem_limit_bytes=VMEM_LIMIT_BYTES)


def _softplus_neg_abs(z):
    return jnp.log(1.0 + jnp.exp(-jnp.abs(z)))


def _log_sigmoid(z):
    return jnp.minimum(z, 0.0) - _softplus_neg_abs(z)


def _sigmoid(z):
    return 1.0 / (1.0 + jnp.exp(-z))


def _round_robin(gens):
    results = [None] * len(gens)
    active = list(enumerate(gens))
    while active:
        still = []
        for i, gen in active:
            try:
                next(gen)
                still.append((i, gen))
            except StopIteration as stop:
                results[i] = stop.value
        active = still
    return results


_ACTS = {
    "none": lambda r: r,
    "relu2": lambda r: jnp.square(jnp.maximum(r, 0.0)),
    "sigmoid": _sigmoid,
    "tanh": jnp.tanh,
}


def _mm_kernel(a_ref, w_ref, o_ref, *scratch, act, nk, w_t):
    w_contract = 1 if w_t else 0
    part = lax.dot_general(a_ref[...].astype(BF16), w_ref[...].astype(BF16), (((1,), (w_contract,)), ((), ())),
                           preferred_element_type=F32)
    if nk == 1:
        o_ref[...] = _ACTS[act](part).astype(o_ref.dtype)
        return
    acc_ref, = scratch
    k = pl.program_id(2)

    @pl.when(k == 0)
    def _():
        acc_ref[...] = part

    @pl.when(jnp.logical_and(k > 0, k < nk - 1))
    def _():
        acc_ref[...] += part

    @pl.when(k == nk - 1)
    def _():
        o_ref[...] = _ACTS[act](acc_ref[...] + part).astype(o_ref.dtype)


def matmul(a, w, *, lead=(), n=None, w_t=False, act="none", out_dtype=F32, tm=1024, tn=None, tk=4096,
           name="matmul"):
    m, k = a.shape
    k2, n_w = w.shape[len(lead):][::-1] if w_t else w.shape[len(lead):]
    n = n_w if n is None else n
    assert k == k2 and n <= n_w and w.ndim == len(lead) + 2
    if tn is None:
        tn = min(1024, MM_TILE_BYTES // (min(tk, k) * w.dtype.itemsize))
    tm, tn = min(tm, m), min(tn, n)
    tk = min(tk, k, MM_TILE_BYTES // (tm * a.dtype.itemsize))
    assert m % tm == 0 and n % tn == 0 and k % tk == 0, (a.shape, w.shape)
    nk = k // tk
    cols_outer = w.dtype == F32 and nk == 1
    ij = (lambda g0, g1: (g1, g0)) if cols_outer else (lambda g0, g1: (g0, g1))
    if w_t:
        w_spec = pl.BlockSpec((None,) * len(lead) + (tn, tk), lambda g0, g1, kk: (*lead, ij(g0, g1)[1], kk))
    else:
        w_spec = pl.BlockSpec((None,) * len(lead) + (tk, tn), lambda g0, g1, kk: (*lead, kk, ij(g0, g1)[1]))
    return pl.pallas_call(
        functools.partial(_mm_kernel, act=act, nk=nk, w_t=w_t),
        grid=(n // tn, m // tm, nk) if cols_outer else (m // tm, n // tn, nk),
        in_specs=[pl.BlockSpec((tm, tk), lambda g0, g1, kk: (ij(g0, g1)[0], kk)), w_spec],
        out_specs=pl.BlockSpec((tm, tn), lambda g0, g1, kk: ij(g0, g1)),
        out_shape=jax.ShapeDtypeStruct((m, n), out_dtype),
        scratch_shapes=[pltpu.VMEM((tm, tn), F32)] if nk > 1 else [],
        compiler_params=_params("parallel", "parallel", "arbitrary"),
        name=name,
    )(a, w)


def _rms(x, g):
    return x * lax.rsqrt(jnp.mean(x * x, axis=-1, keepdims=True) + EPS) * g


def _rms_kernel(x_ref, g_ref, o_ref):
    o_ref[...] = _rms(x_ref[...], g_ref[...]).astype(o_ref.dtype)


def rms_norm(x, g, *, out_dtype=BF16, tr=256):
    m, d = x.shape
    tr = min(tr, m)
    return pl.pallas_call(
        _rms_kernel,
        grid=(m // tr,),
        in_specs=[pl.BlockSpec((tr, d), lambda i: (i, 0)), pl.BlockSpec((1, d), lambda i: (0, 0))],
        out_specs=pl.BlockSpec((tr, d), lambda i: (i, 0)),
        out_shape=jax.ShapeDtypeStruct((m, d), out_dtype),
        compiler_params=_params("parallel"),
        name="rms_norm",
    )(x, g.reshape(1, d))


def _add_rms_kernel(h_ref, u_ref, gpost_ref, *rest, with_pre):
    hn = h_ref[...] + _rms(u_ref[...].astype(F32), gpost_ref[...])
    if with_pre:
        gpre_ref, hn_ref, y_ref = rest
        y_ref[...] = _rms(hn, gpre_ref[...]).astype(y_ref.dtype)
    else:
        hn_ref, = rest
    hn_ref[...] = hn


def add_rms(h, u, g_post, g_pre=None, *, pre_dtype=BF16, tr=128):
    m, d = h.shape
    tr = min(tr, m)
    with_pre = g_pre is not None
    row = pl.BlockSpec((tr, d), lambda i: (i, 0))
    vec = pl.BlockSpec((1, d), lambda i: (0, 0))
    args = [h, u, g_post.reshape(1, d)] + ([g_pre.reshape(1, d)] if with_pre else [])
    out_shape = [jax.ShapeDtypeStruct((m, d), F32)] + ([jax.ShapeDtypeStruct((m, d), pre_dtype)] if with_pre else [])
    outs = pl.pallas_call(
        functools.partial(_add_rms_kernel, with_pre=with_pre),
        grid=(m // tr,),
        in_specs=[row, row, vec] + ([vec] if with_pre else []),
        out_specs=[row] * len(out_shape),
        out_shape=out_shape,
        compiler_params=_params("parallel"),
        name="add_rms",
    )(*args)
    return outs if with_pre else (outs[0], None)


def _mlstm_gate_kernel(x_ref, w_ref, b_ref, o_ref):
    row_w = lax.broadcasted_iota(jnp.int32, w_ref.shape, 0)
    w = jnp.where(row_w < 2 * A_HEADS, w_ref[...], 0.0).astype(BF16)
    g = lax.dot_general(x_ref[...], w, (((1,), (1,)), ((), ())), preferred_element_type=F32) + b_ref[...]
    lane = lax.broadcasted_iota(jnp.int32, g.shape, 1)
    ig = A_GATE_CAP * jnp.tanh(g / A_GATE_CAP)
    o_ref[...] = jnp.where(lane < A_HEADS, ig, _log_sigmoid(g))


def mlstm_gates(x, w_in_t, b_if, idx, *, tr=512):
    m, d = x.shape
    tr = min(tr, m)
    ng = 128
    n_main = w_in_t.shape[1] - 2 * A_HEADS
    assert n_main % ng == 0
    b = jnp.zeros((1, ng), F32).at[0, :2 * A_HEADS].set(b_if[idx].astype(F32))
    return pl.pallas_call(
        _mlstm_gate_kernel,
        grid=(m // tr,),
        in_specs=[pl.BlockSpec((tr, d), lambda i: (i, 0)),
                  pl.BlockSpec((None, ng, d), lambda i: (idx, n_main // ng, 0)),
                  pl.BlockSpec((1, ng), lambda i: (0, 0))],
        out_specs=pl.BlockSpec((tr, ng), lambda i: (i, 0)),
        out_shape=jax.ShapeDtypeStruct((m, ng), F32),
        compiler_params=_params("parallel"),
        name="mlstm_gates",
    )(x, w_in_t, b)


def _mlstm_kernel(q_ref, k_ref, v_ref, og_ref, igr_ref, lfr_ref, g_ref,
                  out_ref, c_ref, n_ref, m_ref, *, L, hg, qk_scale):
    @pl.when(pl.program_id(2) == 0)
    def _():
        c_ref[...] = jnp.zeros_like(c_ref)
        n_ref[...] = jnp.zeros_like(n_ref)
        m_ref[...] = jnp.zeros_like(m_ref)

    dqk = q_ref.shape[1] // hg
    dv = v_ref.shape[1] // hg
    row = lax.broadcasted_iota(jnp.int32, (L, L), 0)
    col = lax.broadcasted_iota(jnp.int32, (L, L), 1)
    causal = col <= row
    tri = causal.astype(F32)
    tri_t = (row <= col).astype(F32)
    eye = row == col

    def head(h):
        qk_cols = slice(h * dqk, (h + 1) * dqk)
        v_cols = slice(h * dv, (h + 1) * dv)
        q = q_ref[:, qk_cols] * qk_scale
        k = k_ref[:, qk_cols]
        v = v_ref[:, v_cols]
        ig_row, lf_row = igr_ref[h], lfr_ref[h]
        ig_col = jnp.sum(jnp.where(eye, ig_row, 0.0), axis=1, keepdims=True)
        lf_col = jnp.sum(jnp.where(eye, lf_row, 0.0), axis=1, keepdims=True)
        m_prev = m_ref[h]

        b_col = jnp.sum(tri * lf_row, axis=1, keepdims=True)
        b_row = jnp.sum(tri_t * lf_col, axis=0, keepdims=True)
        b_last = jnp.sum(lf_row, axis=1, keepdims=True)

        d = jnp.where(causal, b_col - b_row + ig_row, -jnp.inf)
        inter = b_col + m_prev
        m_t = jnp.maximum(jnp.max(d, axis=1, keepdims=True), inter)
        qk = lax.dot_general(q, k, (((1,), (1,)), ((), ())), preferred_element_type=F32)
        yield
        s = qk * jnp.exp(d - m_t)
        inter_w = jnp.exp(inter - m_t)
        c_prev = c_ref[h]
        n_prev = n_ref[h]
        sv = jnp.dot(s.astype(BF16), v, preferred_element_type=F32)
        yield
        qc = jnp.dot(q, c_prev.astype(BF16), preferred_element_type=F32)
        yield
        num = sv + inter_w * qc
        qn = jnp.sum(q.astype(F32) * n_prev, axis=1, keepdims=True)
        den = jnp.sum(s, axis=1, keepdims=True) + inter_w * qn
        hid = num / jnp.maximum(jnp.abs(den), jnp.exp(-m_t))

        a_col = b_last - b_col + ig_col
        a_row = b_last - b_row + ig_row
        m_new = jnp.maximum(b_last + m_prev, jnp.max(a_row, axis=1, keepdims=True))
        w_col = jnp.exp(a_col - m_new)
        decay = jnp.exp(b_last + m_prev - m_new)
        kw = k.astype(F32) * w_col
        kv = lax.dot_general(kw.astype(BF16), v, (((0,), (0,)), ((), ())), preferred_element_type=F32)
        yield
        c_new = decay * c_prev + kv
        n_new = decay * n_prev + jnp.sum(kw, axis=0, keepdims=True)

        hn = hid * lax.rsqrt(jnp.mean(hid * hid, axis=1, keepdims=True) + EPS) * g_ref[:, v_cols]
        out = (hn * _sigmoid(og_ref[:, v_cols].astype(F32))).astype(out_ref.dtype)
        return c_new, n_new, m_new, out

    for h, (c_new, n_new, m_new, out) in enumerate(_round_robin([head(h) for h in range(hg)])):
        c_ref[h] = c_new
        n_ref[h] = n_new
        m_ref[h] = m_new
        out_ref[:, h * dv:(h + 1) * dv] = out


def mlstm_scan(proj, gates, norm_g, *, bsz, seq, L=128, hg=8):
    h_ = A_HEADS
    d_model = norm_g.shape[0]
    dv = d_model // h_
    dqk = dv // 2
    nc = seq // L
    ng = h_ // hg
    g4 = gates[:, :2 * h_].reshape(bsz, nc, L, 2 * h_).transpose(0, 3, 1, 2)
    ig, lf = g4[:, :h_], g4[:, h_:]
    rowv = lambda t: t[..., None, :]
    row_spec = pl.BlockSpec((None, hg, None, 1, L), lambda b, h, c: (b, h, c, 0, 0))
    kq = 2 * h_ * dqk // dv
    return pl.pallas_call(
        functools.partial(_mlstm_kernel, L=L, hg=hg, qk_scale=dqk ** -0.5),
        grid=(bsz, ng, nc),
        in_specs=[pl.BlockSpec((L, hg * dqk), lambda b, h, c: (b * nc + c, h)),
                  pl.BlockSpec((L, hg * dqk), lambda b, h, c: (b * nc + c, ng + h)),
                  pl.BlockSpec((L, hg * dv), lambda b, h, c: (b * nc + c, kq // hg + h)),
                  pl.BlockSpec((L, hg * dv), lambda b, h, c: (b * nc + c, (kq + h_) // hg + h)),
                  row_spec, row_spec,
                  pl.BlockSpec((1, hg * dv), lambda b, h, c: (0, h))],
        out_specs=pl.BlockSpec((L, hg * dv), lambda b, h, c: (b * nc + c, h)),
        out_shape=jax.ShapeDtypeStruct((bsz * seq, d_model), BF16),
        scratch_shapes=[pltpu.VMEM((hg, dqk, dv), F32), pltpu.VMEM((hg, 1, dqk), F32),
                        pltpu.VMEM((hg, 1, 1), F32)],
        compiler_params=_params("parallel", "parallel", "arbitrary"),
        name="mlstm_scan",
    )(proj, proj, proj, proj, rowv(ig), rowv(lf), norm_g.reshape(1, d_model).astype(F32))


def mlstm_mixer(x, w_in, b_if, norm_g, w_out, idx, *, bsz, seq):
    n_main = w_in.shape[2] - 2 * A_HEADS
    w_in_t = jnp.swapaxes(w_in, 1, 2)
    proj = matmul(x, w_in_t, lead=(idx,), n=n_main, w_t=True, out_dtype=BF16, name="mlstm_in")
    gates = mlstm_gates(x, w_in_t, b_if, idx)
    hg = mlstm_scan(proj, gates, norm_g[idx], bsz=bsz, seq=seq)
    return matmul(hg, w_out, lead=(idx,), out_dtype=BF16, name="mlstm_out")


def _sb_kernel(q_ref, k_ref, v_ref, o_ref, *, scale, hg):
    T = Q_BLOCK
    dh = q_ref.shape[1] // hg
    qi = pl.program_id(2)
    row = lax.broadcasted_iota(jnp.int32, (T, T), 0)
    col = lax.broadcasted_iota(jnp.int32, (T, T), 1)
    strict = col < row
    upper = (row > col).astype(BF16)
    upper2 = jnp.concatenate([upper, upper], axis=0)

    def head_block(q, k, v, carry, acc, masked):
        z = lax.dot_general(q, k, (((1,), (1,)), ((), ())), preferred_element_type=F32) * scale
        yield
        sp = _softplus_neg_abs(z)
        ls_pos = jnp.minimum(z, 0.0) - sp
        lk = jnp.minimum(-z, 0.0) - sp
        if masked:
            lk = jnp.where(strict, lk, 0.0)
        hi = lk.astype(BF16)
        lo = (lk - hi.astype(F32)).astype(BF16)
        after = jnp.dot(jnp.concatenate([hi, lo], axis=1), upper2, preferred_element_type=F32)
        yield
        w = jnp.exp(ls_pos + after + carry)
        if masked:
            w = jnp.where(strict, w, 0.0)
        acc = acc + jnp.dot(w.astype(BF16), v, preferred_element_type=F32)
        carry = carry + jnp.sum(lk, axis=1, keepdims=True)
        return carry, acc

    heads = [slice(h * dh, (h + 1) * dh) for h in range(hg)]
    qs = [q_ref[:, sl] for sl in heads]

    def block(kb, carries, accs, masked):
        start = pl.multiple_of(kb * T, T)
        out = _round_robin([head_block(qs[h], k_ref[pl.ds(start, T), heads[h]], v_ref[pl.ds(start, T), heads[h]],
                                       carries[h], accs[h], masked) for h in range(hg)])
        return tuple(o[0] for o in out), tuple(o[1] for o in out)

    carries, accs = block(qi, (jnp.zeros((T, 1), F32),) * hg, (jnp.zeros((T, dh), F32),) * hg, True)

    def live(carries):
        worst = functools.reduce(jnp.maximum, carries)
        return jnp.max(worst) >= SB_UNDERFLOW

    def cond(state):
        j, alive, _, _ = state
        return jnp.logical_and(j < qi, alive)

    def body(state):
        j, _, carries, accs = state
        carries, accs = block(qi - 1 - j, carries, accs, False)
        return j + 1, live(carries), carries, accs

    _, _, _, accs = lax.while_loop(cond, body, (jnp.int32(0), live(carries), carries, accs))
    for h in range(hg):
        o_ref[:, heads[h]] = accs[h].astype(o_ref.dtype)


def stick_breaking_scan(qkv, *, bsz, seq, hg=8):
    d_model = qkv.shape[1] // 3
    dh = d_model // B_HEADS
    nq = seq // Q_BLOCK
    ng = B_HEADS // hg
    wide = hg * dh
    return pl.pallas_call(
        functools.partial(_sb_kernel, scale=dh ** -0.5, hg=hg),
        grid=(bsz, ng, nq),
        in_specs=[pl.BlockSpec((Q_BLOCK, wide), lambda b, h, i: (b * nq + i, h)),
                  pl.BlockSpec((seq, wide), lambda b, h, i: (b, ng + h)),
                  pl.BlockSpec((seq, wide), lambda b, h, i: (b, 2 * ng + h))],
        out_specs=pl.BlockSpec((Q_BLOCK, wide), lambda b, h, i: (b * nq + i, h)),
        out_shape=jax.ShapeDtypeStruct((bsz * seq, d_model), BF16),
        compiler_params=_params("parallel", "parallel", "arbitrary"),
        name="stick_breaking_scan",
    )(qkv, qkv, qkv)


def stick_breaking_mixer(x, w_in, w_out, idx, *, bsz, seq):
    qkv = matmul(x, w_in, lead=(idx,), out_dtype=BF16, name="sb_in")
    o = stick_breaking_scan(qkv, bsz=bsz, seq=seq)
    return matmul(o, w_out, lead=(idx,), out_dtype=BF16, name="sb_out")


def _shift_kernel(x_ref, prev_ref, mu_ref, *o_refs, tiles_per_seq):
    x = x_ref[...]
    first = pl.program_id(0) % tiles_per_seq == 0
    prev_last = jnp.where(first, 0.0, prev_ref[7:8, :])
    row = lax.broadcasted_iota(jnp.int32, x.shape, 0)
    x_prev = jnp.where(row == 0, prev_last, pltpu.roll(x, 1, 0))
    xx = x_prev - x
    for i, o_ref in enumerate(o_refs):
        o_ref[...] = (x + xx * mu_ref[i:i + 1, :]).astype(o_ref.dtype)


def token_shift_mix(x, mu, *, seq, tr=256):
    m, d = x.shape
    tr = min(tr, seq)
    n_out = mu.shape[0]
    row = pl.BlockSpec((tr, d), lambda i: (i, 0))
    return pl.pallas_call(
        functools.partial(_shift_kernel, tiles_per_seq=seq // tr),
        grid=(m // tr,),
        in_specs=[row, pl.BlockSpec((8, d), lambda i: (jnp.maximum(i * (tr // 8) - 1, 0), 0)),
                  pl.BlockSpec((n_out, d), lambda i: (0, 0))],
        out_specs=[row] * n_out,
        out_shape=[jax.ShapeDtypeStruct((m, d), BF16)] * n_out,
        compiler_params=_params("parallel"),
        name="token_shift_mix",
    )(x, x, mu)


def _split_bf16(x, terms):
    parts = []
    for _ in range(terms - 1):
        p = x.astype(BF16)
        parts.append(p)
        x = x - p.astype(F32)
    parts.append(x.astype(BF16))
    return parts


def _rwkv_kernel(r_ref, k_ref, v_ref, wl_ref, al_ref, g_ref, w0_ref, a0_ref, kkw_ref, ka_ref, rk_ref,
                 gng_ref, gnb_ref, o_ref, s_ref, *, L, G, SG):
    N = C_HEAD_DIM
    P = 2 * N
    assert L == N and P == 128

    @pl.when(pl.program_id(2) == 0)
    def _():
        s_ref[...] = jnp.zeros_like(s_ref)

    r128 = lax.broadcasted_iota(jnp.int32, (P, P), 0)
    c128 = lax.broadcasted_iota(jnp.int32, (P, P), 1)
    same_head = (r128 // N) == (c128 // N)
    ones_blk = same_head.astype(BF16)
    t_i = lax.broadcasted_iota(jnp.int32, (L, P), 0)
    s_i = lax.broadcasted_iota(jnp.int32, (L, P), 1) % N
    strict = s_i < t_i
    incl = s_i <= t_i
    eye = (s_i == t_i).astype(F32)
    same_block = lambda width: (t_i // width) == (s_i // width)
    base = 8
    diag_base = same_block(base)
    merges = []
    width = base
    while width < L:
        merges.append(jnp.logical_and(same_block(2 * width), jnp.logical_not(same_block(width))))
        width *= 2
    tri = (lax.broadcasted_iota(jnp.int32, (L, L), 1) <= lax.broadcasted_iota(jnp.int32, (L, L), 0)).astype(BF16)
    lane_head0 = lax.broadcasted_iota(jnp.int32, (1, P), 1) < N
    nt = (((1,), (1,)), ((), ()))

    def seg_sum(x):
        return jnp.dot(x.astype(BF16), ones_blk, preferred_element_type=F32)

    def block_diag(x_cat):
        return jnp.where(same_head, jnp.concatenate([x_cat, x_cat], axis=0), 0.0).astype(BF16)

    def mm(a_cat, b_cat):
        return jnp.dot(a_cat.astype(BF16), block_diag(b_cat), preferred_element_type=F32)

    def lanes(g):
        return pl.ds(g * P, P) if isinstance(g, int) else pl.ds(pl.multiple_of(g * P, P), P)

    def chunk(sl, s_blk):
        r = r_ref[:, sl].astype(F32)
        k = k_ref[:, sl]
        v = v_ref[:, sl].astype(F32)
        wraw = wl_ref[:, sl] + w0_ref[:, sl]
        w_log = -(jnp.maximum(-wraw, 0.0) + _softplus_neg_abs(wraw)) - 0.5
        ld = -jnp.exp(w_log)
        cl2 = jnp.dot(tri, jnp.concatenate(_split_bf16(ld, 2), axis=1), preferred_element_type=F32)
        yield
        a = _sigmoid(al_ref[:, sl].astype(F32) + a0_ref[:, sl])
        kkr = k * kkw_ref[:, sl]
        kk_ss = seg_sum(kkr * kkr)
        yield
        kk = kkr * lax.rsqrt(jnp.maximum(kk_ss, 1e-24))
        kh = k * (1.0 + (a - 1.0) * ka_ref[:, sl])
        b = -(kk * a)
        bonus = seg_sum(r * kh * rk_ref[:, sl])
        yield
        cl = cl2[:, :P] + cl2[:, P:]
        cl_last = cl[L - 1:L, :]
        e_inv = jnp.exp(-cl)
        gam = jnp.exp(cl_last)
        e_tail = gam * e_inv
        lhs = jnp.concatenate([kk * jnp.exp(cl - ld), r * jnp.exp(cl)], axis=0)
        rhs = jnp.concatenate([b * e_inv, kh * e_inv], axis=0).astype(BF16)
        res0 = lax.dot_general(jnp.where(lane_head0, lhs, 0.0).astype(BF16), rhs, nt, preferred_element_type=F32)
        yield
        res1 = lax.dot_general(jnp.where(lane_head0, 0.0, lhs).astype(BF16), rhs, nt, preferred_element_type=F32)
        yield
        ls = lax.dot_general(lhs.astype(BF16), s_blk.astype(BF16), nt, preferred_element_type=F32)
        yield
        pair = lambda blk_r, blk_c: jnp.concatenate(
            [res0[blk_r * L:(blk_r + 1) * L, blk_c * L:(blk_c + 1) * L],
             res1[blk_r * L:(blk_r + 1) * L, blk_c * L:(blk_c + 1) * L]], axis=1)
        nb = jnp.where(strict, pair(0, 0), 0.0)
        nk = jnp.where(strict, pair(0, 1), 0.0)
        mb = jnp.where(incl, pair(1, 0), 0.0)
        mk = jnp.where(incl, pair(1, 1), 0.0)
        v_blk = block_diag(v)
        rhs_rows = ls[:L] + jnp.dot(nk.astype(BF16), v_blk, preferred_element_type=F32)
        yield

        nd = jnp.where(diag_base, nb, 0.0)
        n2 = mm(nd, nd)
        yield
        n4 = mm(n2, n2)
        yield
        t_inv = eye + nd + n2 + mm(nd, n2)
        yield
        t_inv = t_inv + mm(t_inv, n4)
        yield
        for joined in merges:
            half = mm(t_inv, jnp.where(joined, nb, 0.0))
            yield
            t_inv = t_inv + mm(half, t_inv)
            yield
        sk = mm(t_inv, rhs_rows)
        yield

        y = ls[L:] + jnp.dot(jnp.concatenate([mb, mk], axis=1).astype(BF16),
                             jnp.concatenate([block_diag(sk), v_blk], axis=0), preferred_element_type=F32)
        yield
        d_s = lax.dot_general(jnp.concatenate([sk, v], axis=0).astype(BF16),
                              jnp.concatenate([b * e_tail, kh * e_tail], axis=0).astype(BF16),
                              (((0,), (0,)), ((), ())), preferred_element_type=F32)
        yield
        s_new = gam * s_blk + jnp.where(same_head, d_s, 0.0)

        mean = seg_sum(y) * (1.0 / N)
        yield
        yc = y - mean
        var = seg_sum(yc * yc) * (1.0 / N)
        yield
        yn = yc * lax.rsqrt(var + C_GN_EPS) * gng_ref[:, sl] + gnb_ref[:, sl]
        return s_new, ((yn + bonus * v) * g_ref[:, sl].astype(F32)).astype(o_ref.dtype)

    def sub_group(first):
        pairs = [first + g for g in range(SG)]
        results = _round_robin([chunk(lanes(p), s_ref[p]) for p in pairs])
        for p, (s_new, out) in zip(pairs, results):
            s_ref[p] = s_new
            o_ref[:, lanes(p)] = out

    if G == SG:
        sub_group(0)
    else:
        def body(sg, carry):
            sub_group(sg * SG)
            return carry

        lax.fori_loop(0, G // SG, body, 0)


def rwkv7_scan(r, k, v, wl, al, g, w0, a0, k_k, k_a, r_k, gn_g, gn_b, *, bsz, seq, G=32, SG=16):
    m, d = r.shape
    L = C_HEAD_DIM
    P = 2 * C_HEAD_DIM
    nc = seq // L
    wide = G * P
    tile = pl.BlockSpec((L, wide), lambda b, j, c: (b * nc + c, j))
    vec = pl.BlockSpec((1, wide), lambda b, j, c: (0, j))
    vecs = [t.reshape(1, d).astype(F32) for t in (w0, a0, k_k, k_a, r_k, gn_g, gn_b)]
    return pl.pallas_call(
        functools.partial(_rwkv_kernel, L=L, G=G, SG=SG),
        grid=(bsz, d // wide, nc),
        in_specs=[tile] * 6 + [vec] * 7,
        out_specs=tile,
        out_shape=jax.ShapeDtypeStruct((m, d), BF16),
        scratch_shapes=[pltpu.VMEM((G, P, P), F32)],
        compiler_params=_params("parallel", "parallel", "arbitrary"),
        name="rwkv7_scan",
    )(r, k, v, wl, al, g, *vecs)


def rwkv7_mixer(x, mu, w_rkv, w0, w1, w2, a0, a1, a2, g1, g2, k_k, k_a, r_k, gn_g, gn_b, w_out, idx, *, bsz, seq):
    xr, xw, xk, xv, xa, xg = token_shift_mix(x, mu, seq=seq)
    r = matmul(xr, w_rkv, lead=(idx, 0), out_dtype=BF16, name="rwkv_r")
    k = matmul(xk, w_rkv, lead=(idx, 1), name="rwkv_k")
    v = matmul(xv, w_rkv, lead=(idx, 2), out_dtype=BF16, name="rwkv_v")
    wl = matmul(matmul(xw, w1.astype(BF16), act="tanh", out_dtype=BF16, name="rwkv_w1"), w2.astype(BF16),
                name="rwkv_w2")
    al = matmul(matmul(xa, a1.astype(BF16), out_dtype=BF16, name="rwkv_a1"), a2.astype(BF16), out_dtype=BF16,
                name="rwkv_a2")
    n_g = g1.shape[1]
    n_gp = -(-n_g // 128) * 128
    g1p = jnp.zeros((g1.shape[0], n_gp), BF16).at[:, :n_g].set(g1.astype(BF16))
    g2p = jnp.zeros((n_gp, g2.shape[1]), BF16).at[:n_g, :].set(g2.astype(BF16))
    g = matmul(matmul(xg, g1p, act="sigmoid", out_dtype=BF16, name="rwkv_g1"), g2p, out_dtype=BF16, name="rwkv_g2")
    y = rwkv7_scan(r, k, v, wl, al, g, w0, a0, k_k, k_a, r_k, gn_g, gn_b, bsz=bsz, seq=seq)
    return matmul(y, w_out, lead=(idx,), out_dtype=BF16, name="rwkv_out")


def kernel(x, norm_g, mlp_w1, mlp_w2, a_w_in, a_b_if, a_norm_g, a_w_out, b_w_in, b_w_out, c_mu, c_w_rkv,
           c_w0, c_w1, c_w2, c_a0, c_a1, c_a2, c_g1, c_g2, c_k_k, c_k_a, c_r_k, c_gn_g, c_gn_b, c_w_out):
    bsz, seq, d = x.shape
    depth = norm_g.shape[0]
    h = x.reshape(bsz * seq, d)
    kinds = [layer % 3 for layer in range(depth)]
    pre_dtype = lambda kind: F32 if kind == 2 else BF16
    y = rms_norm(h, norm_g[0, 0], out_dtype=pre_dtype(kinds[0]))
    mlp_w2_bf16 = mlp_w2.astype(BF16)
    for layer in range(depth):
        kind, idx = kinds[layer], layer // 3
        if kind == 0:
            u = mlstm_mixer(y, a_w_in, a_b_if, a_norm_g, a_w_out, idx, bsz=bsz, seq=seq)
        elif kind == 1:
            u = stick_breaking_mixer(y, b_w_in, b_w_out, idx, bsz=bsz, seq=seq)
        else:
            u = rwkv7_mixer(y, c_mu[idx], c_w_rkv, c_w0[idx], c_w1[idx], c_w2[idx], c_a0[idx], c_a1[idx],
                            c_a2[idx], c_g1[idx], c_g2[idx], c_k_k[idx], c_k_a[idx], c_r_k[idx], c_gn_g[idx],
                            c_gn_b[idx], c_w_out, idx, bsz=bsz, seq=seq)
        h, y = add_rms(h, u, norm_g[layer, 1], norm_g[layer, 2])
        mid = matmul(y, mlp_w1, lead=(layer,), act="relu2", out_dtype=BF16, name="mlp_up")
        u = matmul(mid, mlp_w2_bf16, lead=(layer,), out_dtype=BF16, name="mlp_down")
        last = layer == depth - 1
        h, y = add_rms(h, u, norm_g[layer, 3], None if last else norm_g[layer + 1, 0],
                       pre_dtype=BF16 if last else pre_dtype(kinds[layer + 1]))
    return h.reshape(bsz, seq, d)
```

```python
import functools

import jax
import jax.numpy as jnp
from jax import lax
from jax.experimental import pallas as pl
from jax.experimental.pallas import tpu as pltpu

F32 = jnp.float32
BF16 = jnp.bfloat16

EPS = 1e-6
VMEM_LIMIT_BYTES = 56 * 1024 * 1024
MM_TILE_BYTES = 8 * 1024 * 1024

A_HEADS = 8
A_GATE_CAP = 15.0
B_HEADS = 32
Q_BLOCK = 128
C_HEAD_DIM = 64
C_GN_EPS = 64e-5
SB_UNDERFLOW = -104.0


def _params(*sem):
    return pltpu.CompilerParams(dimension_semantics=sem, vmem_limit_bytes=VMEM_LIMIT_BYTES)


def _softplus_neg_abs(z):
    return jnp.log(1.0 + jnp.exp(-jnp.abs(z)))


def _log_sigmoid(z):
    return jnp.minimum(z, 0.0) - _softplus_neg_abs(z)


def _sigmoid(z):
    return 1.0 / (1.0 + jnp.exp(-z))


def _round_robin(gens):
    results = [None] * len(gens)
    active = list(enumerate(gens))
    while active:
        still = []
        for i, gen in active:
            try:
                next(gen)
                still.append((i, gen))
            except StopIteration as stop:
                results[i] = stop.value
        active = still
    return results


_ACTS = {
    "none": lambda r: r,
    "relu2": lambda r: jnp.square(jnp.maximum(r, 0.0)),
    "sigmoid": _sigmoid,
    "tanh": jnp.tanh,
}


def _mm_kernel(a_ref, w_ref, o_ref, *scratch, act, nk, w_t):
    w_contract = 1 if w_t else 0
    part = lax.dot_general(a_ref[...].astype(BF16), w_ref[...].astype(BF16), (((1,), (w_contract,)), ((), ())),
                           preferred_element_type=F32)
    if nk == 1:
        o_ref[...] = _ACTS[act](part).astype(o_ref.dtype)
        return
    acc_ref, = scratch
    k = pl.program_id(2)

    @pl.when(k == 0)
    def _():
        acc_ref[...] = part

    @pl.when(jnp.logical_and(k > 0, k < nk - 1))
    def _():
        acc_ref[...] += part

    @pl.when(k == nk - 1)
    def _():
        o_ref[...] = _ACTS[act](acc_ref[...] + part).astype(o_ref.dtype)


def matmul(a, w, *, lead=(), n=None, w_t=False, act="none", out_dtype=F32, tm=1024, tn=None, tk=4096,
           name="matmul"):
    m, k = a.shape
    k2, n_w = w.shape[len(lead):][::-1] if w_t else w.shape[len(lead):]
    n = n_w if n is None else n
    assert k == k2 and n <= n_w and w.ndim == len(lead) + 2
    if tn is None:
        tn = min(1024, MM_TILE_BYTES // (min(tk, k) * w.dtype.itemsize))
    tm, tn = min(tm, m), min(tn, n)
    tk = min(tk, k, MM_TILE_BYTES // (tm * a.dtype.itemsize))
    assert m % tm == 0 and n % tn == 0 and k % tk == 0, (a.shape, w.shape)
    nk = k // tk
    if w_t:
        w_spec = pl.BlockSpec((None,) * len(lead) + (tn, tk), lambda i, j, kk: (*lead, j, kk))
    else:
        w_spec = pl.BlockSpec((None,) * len(lead) + (tk, tn), lambda i, j, kk: (*lead, kk, j))
    return pl.pallas_call(
        functools.partial(_mm_kernel, act=act, nk=nk, w_t=w_t),
        grid=(m // tm, n // tn, nk),
        in_specs=[pl.BlockSpec((tm, tk), lambda i, j, kk: (i, kk)), w_spec],
        out_specs=pl.BlockSpec((tm, tn), lambda i, j, kk: (i, j)),
        out_shape=jax.ShapeDtypeStruct((m, n), out_dtype),
        scratch_shapes=[pltpu.VMEM((tm, tn), F32)] if nk > 1 else [],
        compiler_params=_params("parallel", "parallel", "arbitrary"),
        name=name,
    )(a, w)


def _rms(x, g):
    return x * lax.rsqrt(jnp.mean(x * x, axis=-1, keepdims=True) + EPS) * g


def _rms_kernel(x_ref, g_ref, o_ref):
    o_ref[...] = _rms(x_ref[...], g_ref[...]).astype(o_ref.dtype)


def rms_norm(x, g, *, out_dtype=BF16, tr=256):
    m, d = x.shape
    tr = min(tr, m)
    return pl.pallas_call(
        _rms_kernel,
        grid=(m // tr,),
        in_specs=[pl.BlockSpec((tr, d), lambda i: (i, 0)), pl.BlockSpec((1, d), lambda i: (0, 0))],
        out_specs=pl.BlockSpec((tr, d), lambda i: (i, 0)),
        out_shape=jax.ShapeDtypeStruct((m, d), out_dtype),
        compiler_params=_params("parallel"),
        name="rms_norm",
    )(x, g.reshape(1, d))


def _add_rms_kernel(h_ref, u_ref, gpost_ref, *rest, with_pre):
    hn = h_ref[...] + _rms(u_ref[...].astype(F32), gpost_ref[...])
    if with_pre:
        gpre_ref, hn_ref, y_ref = rest
        y_ref[...] = _rms(hn, gpre_ref[...]).astype(y_ref.dtype)
    else:
        hn_ref, = rest
    hn_ref[...] = hn


def add_rms(h, u, g_post, g_pre=None, *, pre_dtype=BF16, tr=128):
    m, d = h.shape
    tr = min(tr, m)
    with_pre = g_pre is not None
    row = pl.BlockSpec((tr, d), lambda i: (i, 0))
    vec = pl.BlockSpec((1, d), lambda i: (0, 0))
    args = [h, u, g_post.reshape(1, d)] + ([g_pre.reshape(1, d)] if with_pre else [])
    out_shape = [jax.ShapeDtypeStruct((m, d), F32)] + ([jax.ShapeDtypeStruct((m, d), pre_dtype)] if with_pre else [])
    outs = pl.pallas_call(
        functools.partial(_add_rms_kernel, with_pre=with_pre),
        grid=(m // tr,),
        in_specs=[row, row, vec] + ([vec] if with_pre else []),
        out_specs=[row] * len(out_shape),
        out_shape=out_shape,
        compiler_params=_params("parallel"),
        name="add_rms",
    )(*args)
    return outs if with_pre else (outs[0], None)


def _add_rms_shift_kernel(h_ref, u_ref, hp_ref, up_ref, gpost_ref, gpre_ref, mu_ref, hn_ref, *o_refs, tiles_per_seq):
    gpost, gpre = gpost_ref[...], gpre_ref[...]
    hn = h_ref[...] + _rms(u_ref[...].astype(F32), gpost)
    hn_ref[...] = hn
    y = _rms(hn, gpre)
    y_before = _rms(hp_ref[7:8, :] + _rms(up_ref[7:8, :].astype(F32), gpost), gpre)
    y_before = jnp.where(pl.program_id(0) % tiles_per_seq == 0, 0.0, y_before)
    row = lax.broadcasted_iota(jnp.int32, y.shape, 0)
    xx = jnp.where(row == 0, y_before, pltpu.roll(y, 1, 0)) - y
    for i, o_ref in enumerate(o_refs):
        o_ref[...] = (y + xx * mu_ref[i:i + 1, :]).astype(o_ref.dtype)


def add_rms_shift(h, u, g_post, g_pre, mu, *, seq, tr=128):
    m, d = h.shape
    tr = min(tr, seq)
    n_out = mu.shape[0]
    row = pl.BlockSpec((tr, d), lambda i: (i, 0))
    before = pl.BlockSpec((8, d), lambda i: (jnp.maximum(i * (tr // 8) - 1, 0), 0))
    vec = pl.BlockSpec((1, d), lambda i: (0, 0))
    outs = pl.pallas_call(
        functools.partial(_add_rms_shift_kernel, tiles_per_seq=seq // tr),
        grid=(m // tr,),
        in_specs=[row, row, before, before, vec, vec, pl.BlockSpec((n_out, d), lambda i: (0, 0))],
        out_specs=[row] * (1 + n_out),
        out_shape=[jax.ShapeDtypeStruct((m, d), F32)] + [jax.ShapeDtypeStruct((m, d), BF16)] * n_out,
        compiler_params=_params("parallel"),
        name="add_rms_shift",
    )(h, u, h, u, g_post.reshape(1, d), g_pre.reshape(1, d), mu)
    return outs[0], tuple(outs[1:])


def _mlstm_gate_kernel(x_ref, w_ref, b_ref, o_ref):
    row_w = lax.broadcasted_iota(jnp.int32, w_ref.shape, 0)
    w = jnp.where(row_w < 2 * A_HEADS, w_ref[...], 0.0).astype(BF16)
    g = lax.dot_general(x_ref[...], w, (((1,), (1,)), ((), ())), preferred_element_type=F32) + b_ref[...]
    lane = lax.broadcasted_iota(jnp.int32, g.shape, 1)
    ig = A_GATE_CAP * jnp.tanh(g / A_GATE_CAP)
    o_ref[...] = jnp.where(lane < A_HEADS, ig, _log_sigmoid(g))


def mlstm_gates(x, w_in_t, b_if, idx, *, tr=512):
    m, d = x.shape
    tr = min(tr, m)
    ng = 128
    n_main = w_in_t.shape[1] - 2 * A_HEADS
    assert n_main % ng == 0
    b = jnp.zeros((1, ng), F32).at[0, :2 * A_HEADS].set(b_if[idx].astype(F32))
    return pl.pallas_call(
        _mlstm_gate_kernel,
        grid=(m // tr,),
        in_specs=[pl.BlockSpec((tr, d), lambda i: (i, 0)),
                  pl.BlockSpec((None, ng, d), lambda i: (idx, n_main // ng, 0)),
                  pl.BlockSpec((1, ng), lambda i: (0, 0))],
        out_specs=pl.BlockSpec((tr, ng), lambda i: (i, 0)),
        out_shape=jax.ShapeDtypeStruct((m, ng), F32),
        compiler_params=_params("parallel"),
        name="mlstm_gates",
    )(x, w_in_t, b)


def _mlstm_kernel(q_ref, k_ref, v_ref, og_ref, igr_ref, lfr_ref, g_ref,
                  out_ref, c_ref, n_ref, m_ref, *, L, hg, qk_scale):
    @pl.when(pl.program_id(2) == 0)
    def _():
        c_ref[...] = jnp.zeros_like(c_ref)
        n_ref[...] = jnp.zeros_like(n_ref)
        m_ref[...] = jnp.zeros_like(m_ref)

    dqk = q_ref.shape[1] // hg
    dv = v_ref.shape[1] // hg
    row = lax.broadcasted_iota(jnp.int32, (L, L), 0)
    col = lax.broadcasted_iota(jnp.int32, (L, L), 1)
    causal = col <= row
    tri = causal.astype(F32)
    tri_t = (row <= col).astype(F32)
    eye = row == col

    def head(h):
        qk_cols = slice(h * dqk, (h + 1) * dqk)
        v_cols = slice(h * dv, (h + 1) * dv)
        q = q_ref[:, qk_cols] * qk_scale
        k = k_ref[:, qk_cols]
        v = v_ref[:, v_cols]
        ig_row, lf_row = igr_ref[h], lfr_ref[h]
        ig_col = jnp.sum(jnp.where(eye, ig_row, 0.0), axis=1, keepdims=True)
        lf_col = jnp.sum(jnp.where(eye, lf_row, 0.0), axis=1, keepdims=True)
        m_prev = m_ref[h]

        b_col = jnp.sum(tri * lf_row, axis=1, keepdims=True)
        b_row = jnp.sum(tri_t * lf_col, axis=0, keepdims=True)
        b_last = jnp.sum(lf_row, axis=1, keepdims=True)

        d = jnp.where(causal, b_col - b_row + ig_row, -jnp.inf)
        inter = b_col + m_prev
        m_t = jnp.maximum(jnp.max(d, axis=1, keepdims=True), inter)
        qk = lax.dot_general(q, k, (((1,), (1,)), ((), ())), preferred_element_type=F32)
        yield
        s = qk * jnp.exp(d - m_t)
        inter_w = jnp.exp(inter - m_t)
        c_prev = c_ref[h]
        n_prev = n_ref[h]
        sv = jnp.dot(s.astype(BF16), v, preferred_element_type=F32)
        yield
        qc = jnp.dot(q, c_prev.astype(BF16), preferred_element_type=F32)
        yield
        num = sv + inter_w * qc
        qn = jnp.sum(q.astype(F32) * n_prev, axis=1, keepdims=True)
        den = jnp.sum(s, axis=1, keepdims=True) + inter_w * qn
        hid = num / jnp.maximum(jnp.abs(den), jnp.exp(-m_t))

        a_col = b_last - b_col + ig_col
        a_row = b_last - b_row + ig_row
        m_new = jnp.maximum(b_last + m_prev, jnp.max(a_row, axis=1, keepdims=True))
        w_col = jnp.exp(a_col - m_new)
        decay = jnp.exp(b_last + m_prev - m_new)
        kw = k.astype(F32) * w_col
        kv = lax.dot_general(kw.astype(BF16), v, (((0,), (0,)), ((), ())), preferred_element_type=F32)
        yield
        c_new = decay * c_prev + kv
        n_new = decay * n_prev + jnp.sum(kw, axis=0, keepdims=True)

        hn = hid * lax.rsqrt(jnp.mean(hid * hid, axis=1, keepdims=True) + EPS) * g_ref[:, v_cols]
        out = (hn * _sigmoid(og_ref[:, v_cols].astype(F32))).astype(out_ref.dtype)
        return c_new, n_new, m_new, out

    for h, (c_new, n_new, m_new, out) in enumerate(_round_robin([head(h) for h in range(hg)])):
        c_ref[h] = c_new
        n_ref[h] = n_new
        m_ref[h] = m_new
        out_ref[:, h * dv:(h + 1) * dv] = out


def mlstm_scan(proj, gates, norm_g, *, bsz, seq, L=128, hg=8):
    h_ = A_HEADS
    d_model = norm_g.shape[0]
    dv = d_model // h_
    dqk = dv // 2
    nc = seq // L
    ng = h_ // hg
    g4 = gates[:, :2 * h_].reshape(bsz, nc, L, 2 * h_).transpose(0, 3, 1, 2)
    ig, lf = g4[:, :h_], g4[:, h_:]
    rowv = lambda t: t[..., None, :]
    row_spec = pl.BlockSpec((None, hg, None, 1, L), lambda b, h, c: (b, h, c, 0, 0))
    kq = 2 * h_ * dqk // dv
    return pl.pallas_call(
        functools.partial(_mlstm_kernel, L=L, hg=hg, qk_scale=dqk ** -0.5),
        grid=(bsz, ng, nc),
        in_specs=[pl.BlockSpec((L, hg * dqk), lambda b, h, c: (b * nc + c, h)),
                  pl.BlockSpec((L, hg * dqk), lambda b, h, c: (b * nc + c, ng + h)),
                  pl.BlockSpec((L, hg * dv), lambda b, h, c: (b * nc + c, kq // hg + h)),
                  pl.BlockSpec((L, hg * dv), lambda b, h, c: (b * nc + c, (kq + h_) // hg + h)),
                  row_spec, row_spec,
                  pl.BlockSpec((1, hg * dv), lambda b, h, c: (0, h))],
        out_specs=pl.BlockSpec((L, hg * dv), lambda b, h, c: (b * nc + c, h)),
        out_shape=jax.ShapeDtypeStruct((bsz * seq, d_model), BF16),
        scratch_shapes=[pltpu.VMEM((hg, dqk, dv), F32), pltpu.VMEM((hg, 1, dqk), F32),
                        pltpu.VMEM((hg, 1, 1), F32)],
        compiler_params=_params("parallel", "parallel", "arbitrary"),
        name="mlstm_scan",
    )(proj, proj, proj, proj, rowv(ig), rowv(lf), norm_g.reshape(1, d_model).astype(F32))


def mlstm_mixer(x, w_in, b_if, norm_g, w_out, idx, *, bsz, seq):
    n_main = w_in.shape[2] - 2 * A_HEADS
    w_in_t = jnp.swapaxes(w_in, 1, 2)
    proj = matmul(x, w_in_t, lead=(idx,), n=n_main, w_t=True, out_dtype=BF16, name="mlstm_in")
    gates = mlstm_gates(x, w_in_t, b_if, idx)
    hg = mlstm_scan(proj, gates, norm_g[idx], bsz=bsz, seq=seq)
    return matmul(hg, w_out, lead=(idx,), out_dtype=BF16, name="mlstm_out")


def _sb_kernel(q_ref, k_ref, v_ref, o_ref, *, scale, hg):
    T = Q_BLOCK
    dh = q_ref.shape[1] // hg
    qi = pl.program_id(2)
    row = lax.broadcasted_iota(jnp.int32, (T, T), 0)
    col = lax.broadcasted_iota(jnp.int32, (T, T), 1)
    strict = col < row
    upper = (row > col).astype(BF16)
    upper2 = jnp.concatenate([upper, upper], axis=0)

    def head_block(q, k, v, carry, acc, masked):
        z = lax.dot_general(q, k, (((1,), (1,)), ((), ())), preferred_element_type=F32) * scale
        yield
        sp = _softplus_neg_abs(z)
        ls_pos = jnp.minimum(z, 0.0) - sp
        lk = jnp.minimum(-z, 0.0) - sp
        if masked:
            lk = jnp.where(strict, lk, 0.0)
        hi = lk.astype(BF16)
        lo = (lk - hi.astype(F32)).astype(BF16)
        after = jnp.dot(jnp.concatenate([hi, lo], axis=1), upper2, preferred_element_type=F32)
        yield
        w = jnp.exp(ls_pos + after + carry)
        if masked:
            w = jnp.where(strict, w, 0.0)
        acc = acc + jnp.dot(w.astype(BF16), v, preferred_element_type=F32)
        carry = carry + jnp.sum(lk, axis=1, keepdims=True)
        return carry, acc

    heads = [slice(h * dh, (h + 1) * dh) for h in range(hg)]
    qs = [q_ref[:, sl] for sl in heads]

    def block(kb, carries, accs, masked):
        start = pl.multiple_of(kb * T, T)
        out = _round_robin([head_block(qs[h], k_ref[pl.ds(start, T), heads[h]], v_ref[pl.ds(start, T), heads[h]],
                                       carries[h], accs[h], masked) for h in range(hg)])
        return tuple(o[0] for o in out), tuple(o[1] for o in out)

    carries, accs = block(qi, (jnp.zeros((T, 1), F32),) * hg, (jnp.zeros((T, dh), F32),) * hg, True)

    def live(carries):
        worst = functools.reduce(jnp.maximum, carries)
        return jnp.max(worst) >= SB_UNDERFLOW

    def cond(state):
        j, alive, _, _ = state
        return jnp.logical_and(j < qi, alive)

    def body(state):
        j, _, carries, accs = state
        carries, accs = block(qi - 1 - j, carries, accs, False)
        return j + 1, live(carries), carries, accs

    _, _, _, accs = lax.while_loop(cond, body, (jnp.int32(0), live(carries), carries, accs))
    for h in range(hg):
        o_ref[:, heads[h]] = accs[h].astype(o_ref.dtype)


def stick_breaking_scan(qkv, *, bsz, seq, hg=8):
    d_model = qkv.shape[1] // 3
    dh = d_model // B_HEADS
    nq = seq // Q_BLOCK
    ng = B_HEADS // hg
    wide = hg * dh
    return pl.pallas_call(
        functools.partial(_sb_kernel, scale=dh ** -0.5, hg=hg),
        grid=(bsz, ng, nq),
        in_specs=[pl.BlockSpec((Q_BLOCK, wide), lambda b, h, i: (b * nq + i, h)),
                  pl.BlockSpec((seq, wide), lambda b, h, i: (b, ng + h)),
                  pl.BlockSpec((seq, wide), lambda b, h, i: (b, 2 * ng + h))],
        out_specs=pl.BlockSpec((Q_BLOCK, wide), lambda b, h, i: (b * nq + i, h)),
        out_shape=jax.ShapeDtypeStruct((bsz * seq, d_model), BF16),
        compiler_params=_params("parallel", "parallel", "arbitrary"),
        name="stick_breaking_scan",
    )(qkv, qkv, qkv)


def stick_breaking_mixer(x, w_in, w_out, idx, *, bsz, seq):
    qkv = matmul(x, w_in, lead=(idx,), out_dtype=BF16, name="sb_in")
    o = stick_breaking_scan(qkv, bsz=bsz, seq=seq)
    return matmul(o, w_out, lead=(idx,), out_dtype=BF16, name="sb_out")


def _shift_kernel(x_ref, prev_ref, mu_ref, *o_refs, tiles_per_seq):
    x = x_ref[...]
    first = pl.program_id(0) % tiles_per_seq == 0
    prev_last = jnp.where(first, 0.0, prev_ref[7:8, :])
    row = lax.broadcasted_iota(jnp.int32, x.shape, 0)
    x_prev = jnp.where(row == 0, prev_last, pltpu.roll(x, 1, 0))
    xx = x_prev - x
    for i, o_ref in enumerate(o_refs):
        o_ref[...] = (x + xx * mu_ref[i:i + 1, :]).astype(o_ref.dtype)


def token_shift_mix(x, mu, *, seq, tr=256):
    m, d = x.shape
    tr = min(tr, seq)
    n_out = mu.shape[0]
    row = pl.BlockSpec((tr, d), lambda i: (i, 0))
    return pl.pallas_call(
        functools.partial(_shift_kernel, tiles_per_seq=seq // tr),
        grid=(m // tr,),
        in_specs=[row, pl.BlockSpec((8, d), lambda i: (jnp.maximum(i * (tr // 8) - 1, 0), 0)),
                  pl.BlockSpec((n_out, d), lambda i: (0, 0))],
        out_specs=[row] * n_out,
        out_shape=[jax.ShapeDtypeStruct((m, d), BF16)] * n_out,
        compiler_params=_params("parallel"),
        name="token_shift_mix",
    )(x, x, mu)


def _split_bf16(x, terms):
    parts = []
    for _ in range(terms - 1):
        p = x.astype(BF16)
        parts.append(p)
        x = x - p.astype(F32)
    parts.append(x.astype(BF16))
    return parts


def _rwkv_kernel(r_ref, k_ref, v_ref, wl_ref, al_ref, g_ref, w0_ref, a0_ref, kkw_ref, ka_ref, rk_ref,
                 gng_ref, gnb_ref, o_ref, s_ref, *, L, G, SG):
    N = C_HEAD_DIM
    P = 2 * N
    assert L == N and P == 128

    @pl.when(pl.program_id(2) == 0)
    def _():
        s_ref[...] = jnp.zeros_like(s_ref)

    r128 = lax.broadcasted_iota(jnp.int32, (P, P), 0)
    c128 = lax.broadcasted_iota(jnp.int32, (P, P), 1)
    same_head = (r128 // N) == (c128 // N)
    ones_blk = same_head.astype(BF16)
    t_i = lax.broadcasted_iota(jnp.int32, (L, P), 0)
    s_i = lax.broadcasted_iota(jnp.int32, (L, P), 1) % N
    strict = s_i < t_i
    incl = s_i <= t_i
    eye = (s_i == t_i).astype(F32)
    same_block = lambda width: (t_i // width) == (s_i // width)
    base = 8
    diag_base = same_block(base)
    merges = []
    width = base
    while width < L:
        merges.append(jnp.logical_and(same_block(2 * width), jnp.logical_not(same_block(width))))
        width *= 2
    tri = (lax.broadcasted_iota(jnp.int32, (L, L), 1) <= lax.broadcasted_iota(jnp.int32, (L, L), 0)).astype(BF16)
    lane_head0 = lax.broadcasted_iota(jnp.int32, (1, P), 1) < N
    nt = (((1,), (1,)), ((), ()))

    def seg_sum(x):
        return jnp.dot(x.astype(BF16), ones_blk, preferred_element_type=F32)

    def block_diag(x_cat):
        return jnp.where(same_head, jnp.concatenate([x_cat, x_cat], axis=0), 0.0).astype(BF16)

    def mm(a_cat, b_cat):
        return jnp.dot(a_cat.astype(BF16), block_diag(b_cat), preferred_element_type=F32)

    def lanes(g):
        return pl.ds(g * P, P) if isinstance(g, int) else pl.ds(pl.multiple_of(g * P, P), P)

    def chunk(sl, s_blk):
        r = r_ref[:, sl].astype(F32)
        k = k_ref[:, sl]
        v = v_ref[:, sl].astype(F32)
        wraw = wl_ref[:, sl] + w0_ref[:, sl]
        w_log = -(jnp.maximum(-wraw, 0.0) + _softplus_neg_abs(wraw)) - 0.5
        ld = -jnp.exp(w_log)
        cl2 = jnp.dot(tri, jnp.concatenate(_split_bf16(ld, 2), axis=1), preferred_element_type=F32)
        yield
        a = _sigmoid(al_ref[:, sl].astype(F32) + a0_ref[:, sl])
        kkr = k * kkw_ref[:, sl]
        kk_ss = seg_sum(kkr * kkr)
        yield
        kk = kkr * lax.rsqrt(jnp.maximum(kk_ss, 1e-24))
        kh = k * (1.0 + (a - 1.0) * ka_ref[:, sl])
        b = -(kk * a)
        bonus = seg_sum(r * kh * rk_ref[:, sl])
        yield
        cl = cl2[:, :P] + cl2[:, P:]
        cl_last = cl[L - 1:L, :]
        e_inv = jnp.exp(-cl)
        gam = jnp.exp(cl_last)
        e_tail = gam * e_inv
        lhs = jnp.concatenate([kk * jnp.exp(cl - ld), r * jnp.exp(cl)], axis=0)
        rhs = jnp.concatenate([b * e_inv, kh * e_inv], axis=0).astype(BF16)
        res0 = lax.dot_general(jnp.where(lane_head0, lhs, 0.0).astype(BF16), rhs, nt, preferred_element_type=F32)
        yield
        res1 = lax.dot_general(jnp.where(lane_head0, 0.0, lhs).astype(BF16), rhs, nt, preferred_element_type=F32)
        yield
        ls = lax.dot_general(lhs.astype(BF16), s_blk.astype(BF16), nt, preferred_element_type=F32)
        yield
        pair = lambda blk_r, blk_c: jnp.concatenate(
            [res0[blk_r * L:(blk_r + 1) * L, blk_c * L:(blk_c + 1) * L],
             res1[blk_r * L:(blk_r + 1) * L, blk_c * L:(blk_c + 1) * L]], axis=1)
        nb = jnp.where(strict, pair(0, 0), 0.0)
        nk = jnp.where(strict, pair(0, 1), 0.0)
        mb = jnp.where(incl, pair(1, 0), 0.0)
        mk = jnp.where(incl, pair(1, 1), 0.0)
        v_blk = block_diag(v)
        rhs_rows = ls[:L] + jnp.dot(nk.astype(BF16), v_blk, preferred_element_type=F32)
        yield

        nd = jnp.where(diag_base, nb, 0.0)
        n2 = mm(nd, nd)
        yield
        n4 = mm(n2, n2)
        yield
        t_inv = eye + nd + n2 + mm(nd, n2)
        yield
        t_inv = t_inv + mm(t_inv, n4)
        yield
        for joined in merges:
            half = mm(t_inv, jnp.where(joined, nb, 0.0))
            yield
            t_inv = t_inv + mm(half, t_inv)
            yield
        sk = mm(t_inv, rhs_rows)
        yield

        y = ls[L:] + jnp.dot(jnp.concatenate([mb, mk], axis=1).astype(BF16),
                             jnp.concatenate([block_diag(sk), v_blk], axis=0), preferred_element_type=F32)
        yield
        d_s = lax.dot_general(jnp.concatenate([sk, v], axis=0).astype(BF16),
                              jnp.concatenate([b * e_tail, kh * e_tail], axis=0).astype(BF16),
                              (((0,), (0,)), ((), ())), preferred_element_type=F32)
        yield
        s_new = gam * s_blk + jnp.where(same_head, d_s, 0.0)

        mean = seg_sum(y) * (1.0 / N)
        yield
        yc = y - mean
        var = seg_sum(yc * yc) * (1.0 / N)
        yield
        yn = yc * lax.rsqrt(var + C_GN_EPS) * gng_ref[:, sl] + gnb_ref[:, sl]
        return s_new, ((yn + bonus * v) * g_ref[:, sl].astype(F32)).astype(o_ref.dtype)

    def sub_group(first):
        pairs = [first + g for g in range(SG)]
        results = _round_robin([chunk(lanes(p), s_ref[p]) for p in pairs])
        for p, (s_new, out) in zip(pairs, results):
            s_ref[p] = s_new
            o_ref[:, lanes(p)] = out

    if G == SG:
        sub_group(0)
    else:
        def body(sg, carry):
            sub_group(sg * SG)
            return carry

        lax.fori_loop(0, G // SG, body, 0)


def rwkv7_scan(r, k, v, wl, al, g, w0, a0, k_k, k_a, r_k, gn_g, gn_b, *, bsz, seq, G=32, SG=16):
    m, d = r.shape
    L = C_HEAD_DIM
    P = 2 * C_HEAD_DIM
    nc = seq // L
    wide = G * P
    tile = pl.BlockSpec((L, wide), lambda b, j, c: (b * nc + c, j))
    vec = pl.BlockSpec((1, wide), lambda b, j, c: (0, j))
    vecs = [t.reshape(1, d).astype(F32) for t in (w0, a0, k_k, k_a, r_k, gn_g, gn_b)]
    return pl.pallas_call(
        functools.partial(_rwkv_kernel, L=L, G=G, SG=SG),
        grid=(bsz, d // wide, nc),
        in_specs=[tile] * 6 + [vec] * 7,
        out_specs=tile,
        out_shape=jax.ShapeDtypeStruct((m, d), BF16),
        scratch_shapes=[pltpu.VMEM((G, P, P), F32)],
        compiler_params=_params("parallel", "parallel", "arbitrary"),
        name="rwkv7_scan",
    )(r, k, v, wl, al, g, *vecs)


def rwkv7_mixer(x, mu, w_rkv, w0, w1, w2, a0, a1, a2, g1, g2, k_k, k_a, r_k, gn_g, gn_b, w_out, idx, *, bsz, seq):
    xr, xw, xk, xv, xa, xg = x if isinstance(x, tuple) else token_shift_mix(x, mu, seq=seq)
    r = matmul(xr, w_rkv, lead=(idx, 0), out_dtype=BF16, name="rwkv_r")
    k = matmul(xk, w_rkv, lead=(idx, 1), name="rwkv_k")
    v = matmul(xv, w_rkv, lead=(idx, 2), out_dtype=BF16, name="rwkv_v")
    wl = matmul(matmul(xw, w1.astype(BF16), act="tanh", out_dtype=BF16, name="rwkv_w1"), w2.astype(BF16),
                name="rwkv_w2")
    al = matmul(matmul(xa, a1.astype(BF16), out_dtype=BF16, name="rwkv_a1"), a2.astype(BF16), out_dtype=BF16,
                name="rwkv_a2")
    n_g = g1.shape[1]
    n_gp = -(-n_g // 128) * 128
    g1p = jnp.zeros((g1.shape[0], n_gp), BF16).at[:, :n_g].set(g1.astype(BF16))
    g2p = jnp.zeros((n_gp, g2.shape[1]), BF16).at[:n_g, :].set(g2.astype(BF16))
    g = matmul(matmul(xg, g1p, act="sigmoid", out_dtype=BF16, name="rwkv_g1"), g2p, out_dtype=BF16, name="rwkv_g2")
    y = rwkv7_scan(r, k, v, wl, al, g, w0, a0, k_k, k_a, r_k, gn_g, gn_b, bsz=bsz, seq=seq)
    return matmul(y, w_out, lead=(idx,), out_dtype=BF16, name="rwkv_out")


def kernel(x, norm_g, mlp_w1, mlp_w2, a_w_in, a_b_if, a_norm_g, a_w_out, b_w_in, b_w_out, c_mu, c_w_rkv,
           c_w0, c_w1, c_w2, c_a0, c_a1, c_a2, c_g1, c_g2, c_k_k, c_k_a, c_r_k, c_gn_g, c_gn_b, c_w_out):
    bsz, seq, d = x.shape
    depth = norm_g.shape[0]
    h = x.reshape(bsz * seq, d)
    kinds = [layer % 3 for layer in range(depth)]
    pre_dtype = lambda kind: F32 if kind == 2 else BF16
    y = rms_norm(h, norm_g[0, 0], out_dtype=pre_dtype(kinds[0]))
    mlp_w2_bf16 = mlp_w2.astype(BF16)
    for layer in range(depth):
        kind, idx = kinds[layer], layer // 3
        if kind == 0:
            u = mlstm_mixer(y, a_w_in, a_b_if, a_norm_g, a_w_out, idx, bsz=bsz, seq=seq)
        elif kind == 1:
            u = stick_breaking_mixer(y, b_w_in, b_w_out, idx, bsz=bsz, seq=seq)
        else:
            u = rwkv7_mixer(y, c_mu[idx], c_w_rkv, c_w0[idx], c_w1[idx], c_w2[idx], c_a0[idx], c_a1[idx],
                            c_a2[idx], c_g1[idx], c_g2[idx], c_k_k[idx], c_k_a[idx], c_r_k[idx], c_gn_g[idx],
                            c_gn_b[idx], c_w_out, idx, bsz=bsz, seq=seq)
        h, y = add_rms(h, u, norm_g[layer, 1], norm_g[layer, 2])
        mid = matmul(y, mlp_w1, lead=(layer,), act="relu2", out_dtype=BF16, name="mlp_up")
        u = matmul(mid, mlp_w2_bf16, lead=(layer,), out_dtype=BF16, name="mlp_down")
        last = layer == depth - 1
        if not last and kinds[layer + 1] == 2:
            h, y = add_rms_shift(h, u, norm_g[layer, 3], norm_g[layer + 1, 0], c_mu[(layer + 1) // 3], seq=seq)
        else:
            h, y = add_rms(h, u, norm_g[layer, 3], None if last else norm_g[layer + 1, 0],
                           pre_dtype=BF16 if last else pre_dtype(kinds[layer + 1]))
    return h.reshape(bsz, seq, d)
```
